```python
import jax, jax.numpy as jnp
from jax import lax
import numpy as np

D_MODEL = 2048
BATCH = 2
SEQ = 8192
DEPTH = 2

HEAD_DIM = 128
N_SB_HEADS = 12
N_MEM_HEADS = 4
N_MLA_HEADS = 12
MEM_LEN = 256
Q_LORA_RANK = 512
KV_LORA_RANK = 512
QK_NOPE_DIM = 128
QK_ROPE_DIM = 64
V_HEAD_DIM = 128
ROPE_THETA = 10000.0
BLOCK_Q = 128
EPS = 1e-6
N_A_LAYERS = DEPTH // 2
N_B_LAYERS = DEPTH - N_A_LAYERS
SB_W = N_SB_HEADS * HEAD_DIM
MEM_W = N_MEM_HEADS * HEAD_DIM
MLA_W = N_MLA_HEADS * V_HEAD_DIM
A_IN_SIZES = (SB_W, SB_W, SB_W, SB_W, MEM_W, MEM_W)
B_IN_SIZES = (Q_LORA_RANK, MLA_W, MEM_W, MEM_W)
A_IN_W = sum(A_IN_SIZES)
B_IN_W = sum(B_IN_SIZES)
MIX_A_W = SB_W + MEM_W
MIX_B_W = MLA_W + MEM_W

kernel_name = "yoco_stickbreak_mla_memory_hybrid"


def _split(x, sizes):
    idx = [int(i) for i in np.cumsum(sizes)[:-1]]
    return jnp.split(x, idx, axis=-1)


def rmsnorm(x, g):
    xf = x.astype(jnp.float32)
    y = xf * lax.rsqrt(jnp.mean(xf * xf, axis=-1, keepdims=True) + EPS)
    return (y * g.astype(jnp.float32)).astype(x.dtype)


def rope_tables(positions):
    inv_freq = jnp.power(ROPE_THETA, -jnp.arange(0, QK_ROPE_DIM, 2, dtype=jnp.float32) / QK_ROPE_DIM)
    ang = positions.astype(jnp.float32)[..., None] * inv_freq
    return jnp.cos(ang), jnp.sin(ang)


def apply_rope(x, cos, sin):
    half = x.shape[-1] // 2
    x1, x2 = x[..., :half], x[..., half:]
    return jnp.concatenate([x1 * cos - x2 * sin, x2 * cos + x1 * sin], axis=-1).astype(x.dtype)


def _to_blocks(t):
    b, s = t.shape[0], t.shape[1]
    return t.reshape(b, s // BLOCK_Q, BLOCK_Q, *t.shape[2:]).swapaxes(0, 1)


def _from_blocks(t):
    t = t.swapaxes(0, 1)
    return t.reshape(t.shape[0], t.shape[1] * t.shape[2], *t.shape[3:])


def stick_breaking_attention(q, k, v):
    s_len, d = q.shape[1], q.shape[-1]
    scale = d ** -0.5
    key_pos = jnp.arange(s_len)

    def block(args):
        qi, bi = args
        z = jnp.einsum('bqhd,bkhd->bhqk', qi, k).astype(jnp.float32) * scale
        q_pos = bi * BLOCK_Q + jnp.arange(BLOCK_Q)
        causal = key_pos[None, :] < q_pos[:, None]
        log_1m_beta = jnp.where(causal, jax.nn.log_sigmoid(-z), 0.0)
        between = lax.cumsum(log_1m_beta, axis=3, reverse=True) - log_1m_beta
        a = jnp.where(causal, jnp.exp(jax.nn.log_sigmoid(z) + between), 0.0)
        return jnp.einsum('bhqk,bkhd->bqhd', a.astype(v.dtype), v)

    nblk = s_len // BLOCK_Q
    out = lax.map(block, (_to_blocks(q), jnp.arange(nblk)))
    return _from_blocks(out)


def mla_causal_attention(q_nope, q_rope, k_nope, k_rope, v):
    s_len = q_nope.shape[1]
    scale = (QK_NOPE_DIM + QK_ROPE_DIM) ** -0.5
    key_pos = jnp.arange(s_len)

    def block(args):
        qn, qr, bi = args
        s = (jnp.einsum('bqhd,bkhd->bhqk', qn, k_nope)
             + jnp.einsum('bqhr,bkr->bhqk', qr, k_rope)).astype(jnp.float32) * scale
        q_pos = bi * BLOCK_Q + jnp.arange(BLOCK_Q)
        mask = key_pos[None, :] <= q_pos[:, None]
        p = jax.nn.softmax(jnp.where(mask, s, -jnp.inf), axis=-1)
        return jnp.einsum('bhqk,bkhd->bqhd', p.astype(v.dtype), v)

    nblk = s_len // BLOCK_Q
    out = lax.map(block, (_to_blocks(q_nope), _to_blocks(q_rope), jnp.arange(nblk)))
    return _from_blocks(out)


def memory_attention(q_m, mem, g_norm, w_kv, g_q, g_k):
    b, s_len, _ = q_m.shape
    mk, mv = _split(rmsnorm(mem, g_norm) @ w_kv, (MEM_W, MEM_W))
    mk = rmsnorm(mk.reshape(b, -1, N_MEM_HEADS, HEAD_DIM), g_k)
    mv = mv.reshape(b, -1, N_MEM_HEADS, HEAD_DIM)
    q = rmsnorm(q_m.reshape(b, s_len, N_MEM_HEADS, HEAD_DIM), g_q)
    s = jnp.einsum('bqhd,bmhd->bhqm', q, mk).astype(jnp.float32) * HEAD_DIM ** -0.5
    p = jax.nn.softmax(s, axis=-1)
    return jnp.einsum('bhqm,bmhd->bqhd', p.astype(mv.dtype), mv).reshape(b, s_len, MEM_W)


def setup_inputs(seed: int = 0) -> dict:
    key = jax.random.key(seed)
    ks = iter(jax.random.split(key, 32))
    f32 = jnp.float32

    def w(shape, fan_in):
        return jax.random.normal(next(ks), shape, f32) * fan_in ** -0.5

    def gain(shape):
        return 1.0 + 0.02 * jax.random.normal(next(ks), shape, f32)

    x = jax.random.normal(next(ks), (BATCH, SEQ, D_MODEL), f32)
    mem = jax.random.normal(next(ks), (BATCH, MEM_LEN, D_MODEL), f32)
    positions = jnp.broadcast_to(jnp.arange(SEQ, dtype=jnp.int32)[None, :], (BATCH, SEQ))
    return {
        "x": x,
        "mem": mem,
        "positions": positions,
        "a_norm": gain((N_A_LAYERS, D_MODEL)),
        "a_w_in": w((N_A_LAYERS, D_MODEL, A_IN_W), D_MODEL),
        "a_w_out": w((N_A_LAYERS, MIX_A_W, D_MODEL), MIX_A_W),
        "kv_norm": gain((D_MODEL,)),
        "w_dkv": w((D_MODEL, KV_LORA_RANK + QK_ROPE_DIM), D_MODEL),
        "g_ckv": gain((KV_LORA_RANK,)),
        "w_ukv": w((KV_LORA_RANK, N_MLA_HEADS * (QK_NOPE_DIM + V_HEAD_DIM)), KV_LORA_RANK),
        "g_k_nope": gain((QK_NOPE_DIM,)),
        "g_k_rope": gain((QK_ROPE_DIM,)),
        "b_norm": gain((N_B_LAYERS, D_MODEL)),
        "b_w_in": w((N_B_LAYERS, D_MODEL, B_IN_W), D_MODEL),
        "b_g_q_lat": gain((N_B_LAYERS, Q_LORA_RANK)),
        "b_w_uq": w((N_B_LAYERS, Q_LORA_RANK, N_MLA_HEADS * (QK_NOPE_DIM + QK_ROPE_DIM)), Q_LORA_RANK),
        "b_g_q_nope": gain((N_B_LAYERS, QK_NOPE_DIM)),
        "b_g_q_rope": gain((N_B_LAYERS, QK_ROPE_DIM)),
        "b_w_out": w((N_B_LAYERS, MIX_B_W, D_MODEL), MIX_B_W),
        "mem_norm": gain((DEPTH, D_MODEL)),
        "w_mem_kv": w((DEPTH, D_MODEL, 2 * MEM_W), D_MODEL),
        "g_mem_q": gain((DEPTH, HEAD_DIM)),
        "g_mem_k": gain((DEPTH, HEAD_DIM)),
    }


def reference(x, mem, positions, a_norm, a_w_in, a_w_out, kv_norm, w_dkv, g_ckv, w_ukv,
              g_k_nope, g_k_rope, b_norm, b_w_in, b_g_q_lat, b_w_uq, b_g_q_nope, b_g_q_rope,
              b_w_out, mem_norm, w_mem_kv, g_mem_q, g_mem_k):
    b, s_len, _ = x.shape
    cos, sin = rope_tables(positions)
    cos_h, sin_h = cos[:, :, None, :], sin[:, :, None, :]
    shared = None
    for layer in range(DEPTH):
        if layer < N_A_LAYERS:
            i = layer
            h = rmsnorm(x, a_norm[i])
            q, k, v, g_sb, q_m, g_m = _split(h @ a_w_in[i], A_IN_SIZES)
            heads = lambda t: t.reshape(b, s_len, N_SB_HEADS, HEAD_DIM)
            sb = stick_breaking_attention(heads(q), heads(k), heads(v)).reshape(b, s_len, SB_W)
            mo = memory_attention(q_m, mem, mem_norm[layer], w_mem_kv[layer], g_mem_q[layer], g_mem_k[layer])
            mixed = jnp.concatenate([sb * jax.nn.silu(g_sb), mo * jax.nn.silu(g_m)], axis=-1)
            x = x + mixed @ a_w_out[i]
        else:
            j = layer - N_A_LAYERS
            if shared is None:
                c_kv, k_r = _split(rmsnorm(x, kv_norm) @ w_dkv, (KV_LORA_RANK, QK_ROPE_DIM))
                kv = (rmsnorm(c_kv, g_ckv) @ w_ukv).reshape(b, s_len, N_MLA_HEADS, QK_NOPE_DIM + V_HEAD_DIM)
                k_nope = rmsnorm(kv[..., :QK_NOPE_DIM], g_k_nope)
                v_mla = kv[..., QK_NOPE_DIM:]
                k_rope = apply_rope(rmsnorm(k_r, g_k_rope), cos, sin)
                shared = (k_nope, k_rope, v_mla)
            k_nope, k_rope, v_mla = shared
            h = rmsnorm(x, b_norm[j])
            q_lat, g_mla, q_m, g_m = _split(h @ b_w_in[j], B_IN_SIZES)
            q = (rmsnorm(q_lat, b_g_q_lat[j]) @ b_w_uq[j]).reshape(
                b, s_len, N_MLA_HEADS, QK_NOPE_DIM + QK_ROPE_DIM)
            q_nope = rmsnorm(q[..., :QK_NOPE_DIM], b_g_q_nope[j])
            q_rope = apply_rope(rmsnorm(q[..., QK_NOPE_DIM:], b_g_q_rope[j]), cos_h, sin_h)
            att = mla_causal_attention(q_nope, q_rope, k_nope, k_rope, v_mla).reshape(b, s_len, MLA_W)
            mo = memory_attention(q_m, mem, mem_norm[layer], w_mem_kv[layer], g_mem_q[layer], g_mem_k[layer])
            mixed = jnp.concatenate([att * jax.nn.silu(g_mla), mo * jax.nn.silu(g_m)], axis=-1)
            x = x + mixed @ b_w_out[j]
    return x
```

```python
import functools

import jax
import jax.numpy as jnp
import numpy as np
from jax import lax
from jax.experimental import pallas as pl
from jax.experimental.pallas import tpu as pltpu

F32 = jnp.float32
BF16 = jnp.bfloat16

HEAD_DIM = 128
N_SB_HEADS = 12
N_MEM_HEADS = 4
N_MLA_HEADS = 12
Q_LORA_RANK = 512
KV_LORA_RANK = 512
QK_NOPE_DIM = 128
QK_ROPE_DIM = 64
V_HEAD_DIM = 128
ROPE_THETA = 10000.0
EPS = 1e-6
SB_W = N_SB_HEADS * HEAD_DIM
MEM_W = N_MEM_HEADS * HEAD_DIM
MLA_W = N_MLA_HEADS * V_HEAD_DIM
LANES = 128
QK_PAD = 2 * LANES
VMEM_LIMIT = 56 * 1024 * 1024


def _rms_scale(x, width=None):
    ss = jnp.sum(x * x, axis=-1, keepdims=True)
    return lax.rsqrt(ss / (x.shape[-1] if width is None else width) + EPS)


def _silu(g):
    return g / (1.0 + jnp.exp(-g))


def _dot(a, b):
    return jnp.dot(a, b, preferred_element_type=F32)


def _dot_nt(a, b):
    return lax.dot_general(a, b, (((1,), (1,)), ((), ())), preferred_element_type=F32)


def _normmm_kernel(x_ref, g_ref, w_ref, cs_ref, o_ref, h_ref):
    @pl.when(pl.program_id(1) == 0)
    def _():
        x = x_ref[...]
        h_ref[...] = (x * _rms_scale(x) * g_ref[...]).astype(BF16)

    o_ref[...] = (_dot(h_ref[...], w_ref[...]) * cs_ref[...]).astype(o_ref.dtype)


def _norm_matmul(x, g, w, colscale, out_dtype, tm, tn):
    m, k = x.shape
    n = w.shape[1]
    return pl.pallas_call(
        _normmm_kernel,
        grid=(m // tm, n // tn),
        in_specs=[
            pl.BlockSpec((tm, k), lambda i, j: (i, 0)),
            pl.BlockSpec((1, k), lambda i, j: (0, 0)),
            pl.BlockSpec((k, tn), lambda i, j: (0, j)),
            pl.BlockSpec((1, tn), lambda i, j: (0, j)),
        ],
        out_specs=pl.BlockSpec((tm, tn), lambda i, j: (i, j)),
        out_shape=jax.ShapeDtypeStruct((m, n), out_dtype),
        scratch_shapes=[pltpu.VMEM((tm, k), BF16)],
        compiler_params=pltpu.CompilerParams(
            dimension_semantics=("parallel", "arbitrary"), vmem_limit_bytes=VMEM_LIMIT),
        name="norm_matmul",
    )(x, g.reshape(1, k), w, colscale.reshape(1, n))


def _sb_kernel(q_ref, k_ref, v_ref, g_ref, u_ref, o_ref, acc_ref, *, tq):
    i = pl.program_id(2)
    q = q_ref[...]
    u = u_ref[...]

    def scores(kb):
        k = k_ref[pl.ds(pl.multiple_of(kb * tq, tq), tq), :]
        z = _dot_nt(q, k)
        return z, jnp.minimum(-z, 0.0) - jnp.log(1.0 + jnp.exp(-jnp.abs(z)))

    def weights(z, l, carry):
        l_hi = l.astype(BF16)
        l_lo = (l - l_hi.astype(F32)).astype(BF16)
        between = _dot(l_hi, u) + _dot(l_lo, u) + carry
        return jnp.exp(z + l + between)

    def accumulate(kb, a):
        v = v_ref[pl.ds(pl.multiple_of(kb * tq, tq), tq), :]
        return _dot(a.astype(BF16), v)

    z, l = scores(i)
    row = lax.broadcasted_iota(jnp.int32, (tq, tq), 0)
    col = lax.broadcasted_iota(jnp.int32, (tq, tq), 1)
    causal = col < row
    l = jnp.where(causal, l, 0.0)
    a = jnp.where(causal, weights(z, l, jnp.zeros((tq, 1), F32)), 0.0)
    acc_ref[...] = accumulate(i, a)
    carry = jnp.sum(l, axis=-1, keepdims=True)

    def body(j, carry):
        kb = i - 1 - j
        z, l = scores(kb)
        acc_ref[...] += accumulate(kb, weights(z, l, carry))
        return carry + jnp.sum(l, axis=-1, keepdims=True)

    lax.fori_loop(0, i, body, carry)
    o_ref[...] = (acc_ref[...] * _silu(g_ref[...].astype(F32))).astype(o_ref.dtype)


def _sb_attention(pa, batch, seq, tq):
    nq = seq // tq
    h0 = N_SB_HEADS
    u = jnp.tril(jnp.ones((tq, tq), F32), -1).astype(BF16)
    return pl.pallas_call(
        functools.partial(_sb_kernel, tq=tq),
        grid=(batch, N_SB_HEADS, nq),
        in_specs=[
            pl.BlockSpec((tq, HEAD_DIM), lambda b, h, i: (b * nq + i, h)),
            pl.BlockSpec((seq, HEAD_DIM), lambda b, h, i: (b, h0 + h)),
            pl.BlockSpec((seq, HEAD_DIM), lambda b, h, i: (b, 2 * h0 + h)),
            pl.BlockSpec((tq, HEAD_DIM), lambda b, h, i: (b * nq + i, 3 * h0 + h)),
            pl.BlockSpec((tq, tq), lambda b, h, i: (0, 0)),
        ],
        out_specs=pl.BlockSpec((tq, HEAD_DIM), lambda b, h, i: (b * nq + i, h)),
        out_shape=jax.ShapeDtypeStruct((batch * seq, SB_W), BF16),
        scratch_shapes=[pltpu.VMEM((tq, HEAD_DIM), F32)],
        compiler_params=pltpu.CompilerParams(
            dimension_semantics=("parallel", "parallel", "arbitrary"), vmem_limit_bytes=VMEM_LIMIT),
        name="sb_attention",
    )(pa, pa, pa, pa, u)


def _mla_kernel(q_ref, k_ref, v_ref, g_ref, o_ref, acc_ref, *, tq):
    i = pl.program_id(2)
    q = q_ref[...]

    def step(kb, m, den, mask):
        start = pl.multiple_of(kb * tq, tq)
        s = _dot_nt(q, k_ref[pl.ds(start, tq), :])
        if mask is not None:
            s = jnp.where(mask, s, -jnp.inf)
        m_new = jnp.maximum(m, jnp.max(s, axis=-1, keepdims=True))
        alpha = jnp.exp(m - m_new)
        p = jnp.exp(s - m_new)
        pv = _dot(p.astype(BF16), v_ref[pl.ds(start, tq), :])
        acc_ref[...] = alpha * acc_ref[...] + pv
        return m_new, alpha * den + jnp.sum(p, axis=-1, keepdims=True)

    row = lax.broadcasted_iota(jnp.int32, (tq, tq), 0)
    col = lax.broadcasted_iota(jnp.int32, (tq, tq), 1)
    acc_ref[...] = jnp.zeros_like(acc_ref)
    m, den = step(i, jnp.full((tq, 1), -jnp.inf, F32), jnp.zeros((tq, 1), F32), col <= row)
    m, den = lax.fori_loop(0, i, lambda kb, c: step(kb, c[0], c[1], None), (m, den))
    o_ref[...] = (acc_ref[...] / den * _silu(g_ref[...].astype(F32))).astype(o_ref.dtype)


def _mla_attention(qx, kx, v, gq, batch, seq, tq):
    nq = seq // tq
    return pl.pallas_call(
        functools.partial(_mla_kernel, tq=tq),
        grid=(batch, N_MLA_HEADS, nq),
        in_specs=[
            pl.BlockSpec((tq, QK_PAD), lambda b, h, i: (b * nq + i, h)),
            pl.BlockSpec((seq, QK_PAD), lambda b, h, i: (b, h)),
            pl.BlockSpec((seq, V_HEAD_DIM), lambda b, h, i: (b, h)),
            pl.BlockSpec((tq, V_HEAD_DIM), lambda b, h, i: (b * nq + i, h)),
        ],
        out_specs=pl.BlockSpec((tq, V_HEAD_DIM), lambda b, h, i: (b * nq + i, h)),
        out_shape=jax.ShapeDtypeStruct((batch * seq, MLA_W), BF16),
        scratch_shapes=[pltpu.VMEM((tq, V_HEAD_DIM), F32)],
        compiler_params=pltpu.CompilerParams(
            dimension_semantics=("parallel", "parallel", "arbitrary"), vmem_limit_bytes=VMEM_LIMIT),
        name="mla_attention",
    )(qx, kx, v, gq)


def _out_kernel(main_ref, qm_ref, gm_ref, mkv_ref, gq_ref, gk_ref, x_ref, w_ref, o_ref, mo_ref, *, main_w):
    @pl.when(pl.program_id(1) == 0)
    def _():
        for h in range(N_MEM_HEADS):
            lo, hi = h * HEAD_DIM, (h + 1) * HEAD_DIM
            mk = mkv_ref[0, :, lo:hi]
            mk = (mk * _rms_scale(mk) * gk_ref[...]).astype(BF16)
            mv = mkv_ref[0, :, MEM_W + lo:MEM_W + hi].astype(BF16)
            q = qm_ref[:, lo:hi].astype(F32)
            q = (q * _rms_scale(q) * gq_ref[...]).astype(BF16)
            s = _dot_nt(q, mk) * HEAD_DIM ** -0.5
            p = jnp.exp(s - jnp.max(s, axis=-1, keepdims=True))
            mo = _dot(p.astype(BF16), mv) / jnp.sum(p, axis=-1, keepdims=True)
            mo_ref[:, lo:hi] = (mo * _silu(gm_ref[:, lo:hi].astype(F32))).astype(BF16)

    acc = _dot(main_ref[...], w_ref[:main_w, :]) + _dot(mo_ref[...], w_ref[main_w:, :])
    o_ref[...] = x_ref[...] + acc


def _mem_out_proj(main, side, qm_blk, gm_blk, mkv, g_q, g_k, x, w, seq, tm, tn):
    m, main_w = main.shape
    d = w.shape[1]
    mem_len = mkv.shape[1]
    per_b = seq // tm
    return pl.pallas_call(
        functools.partial(_out_kernel, main_w=main_w),
        grid=(m // tm, d // tn),
        in_specs=[
            pl.BlockSpec((tm, main_w), lambda i, j: (i, 0)),
            pl.BlockSpec((tm, MEM_W), lambda i, j: (i, qm_blk)),
            pl.BlockSpec((tm, MEM_W), lambda i, j: (i, gm_blk)),
            pl.BlockSpec((1, mem_len, 2 * MEM_W), lambda i, j: (i // per_b, 0, 0)),
            pl.BlockSpec((1, HEAD_DIM), lambda i, j: (0, 0)),
            pl.BlockSpec((1, HEAD_DIM), lambda i, j: (0, 0)),
            pl.BlockSpec((tm, tn), lambda i, j: (i, j)),
            pl.BlockSpec((main_w + MEM_W, tn), lambda i, j: (0, j)),
        ],
        out_specs=pl.BlockSpec((tm, tn), lambda i, j: (i, j)),
        out_shape=jax.ShapeDtypeStruct((m, d), F32),
        scratch_shapes=[pltpu.VMEM((tm, MEM_W), BF16)],
        compiler_params=pltpu.CompilerParams(
            dimension_semantics=("parallel", "arbitrary"), vmem_limit_bytes=VMEM_LIMIT),
        name="mem_out_proj",
    )(main, side, side, mkv, g_q.reshape(1, HEAD_DIM), g_k.reshape(1, HEAD_DIM), x, w)


def _rope_tables(pos_ref, c_ref):
    ang = pos_ref[...].astype(F32) * c_ref[0:1, :]
    cosv, sinv = jnp.cos(ang), jnp.sin(ang)
    return cosv * c_ref[1:2, :], sinv * c_ref[2:3, :], sinv * c_ref[3:4, :]


def _rope(x, tables):
    c, s1, s2 = tables
    return x * c + pltpu.roll(x, LANES - QK_ROPE_DIM // 2, 1) * s1 + pltpu.roll(x, QK_ROPE_DIM // 2, 1) * s2


def _kv_kernel(x_ref, pos_ref, c_ref, gx_ref, wd_ref, gc_ref, wu_ref, gkn_ref, gkr_ref, kx_ref, v_ref):
    x = x_ref[...]
    h = (x * _rms_scale(x) * gx_ref[...]).astype(BF16)
    c = _dot(h, wd_ref[...])
    ckv = c[:, :KV_LORA_RANK]
    cn = (ckv * _rms_scale(ckv) * gc_ref[...]).astype(BF16)
    kv = _dot(cn, wu_ref[...])
    kr = c[:, KV_LORA_RANK:]
    kr = kr * _rms_scale(kr, QK_ROPE_DIM) * gkr_ref[...]
    k_rope = _rope(kr, _rope_tables(pos_ref, c_ref)).astype(BF16)
    for hd in range(N_MLA_HEADS):
        base = hd * QK_PAD
        kn = kv[:, base:base + QK_NOPE_DIM]
        kx_ref[:, base:base + QK_NOPE_DIM] = (kn * _rms_scale(kn) * gkn_ref[...]).astype(BF16)
        kx_ref[:, base + QK_NOPE_DIM:base + QK_PAD] = k_rope
        v_ref[:, hd * V_HEAD_DIM:(hd + 1) * V_HEAD_DIM] = kv[:, base + QK_NOPE_DIM:base + QK_PAD].astype(BF16)


def _q_kernel(x_ref, pos_ref, c_ref, gx_ref, win_ref, gl_ref, wuq_ref, gqn_ref, gqr_ref, qx_ref, side_ref):
    x = x_ref[...]
    h = (x * _rms_scale(x) * gx_ref[...]).astype(BF16)
    p = _dot(h, win_ref[...])
    side_ref[...] = p[:, Q_LORA_RANK:].astype(BF16)
    ql = p[:, :Q_LORA_RANK]
    qn = (ql * _rms_scale(ql) * gl_ref[...]).astype(BF16)
    q = _dot(qn, wuq_ref[...])
    tables = _rope_tables(pos_ref, c_ref)
    scale = (QK_NOPE_DIM + QK_ROPE_DIM) ** -0.5
    for hd in range(N_MLA_HEADS):
        base = hd * QK_PAD
        qnope = q[:, base:base + QK_NOPE_DIM]
        qx_ref[:, base:base + QK_NOPE_DIM] = (qnope * _rms_scale(qnope) * gqn_ref[...] * scale).astype(BF16)
        qr = q[:, base + QK_NOPE_DIM:base + QK_PAD]
        qr = qr * _rms_scale(qr, QK_ROPE_DIM) * gqr_ref[...]
        qx_ref[:, base + QK_NOPE_DIM:base + QK_PAD] = (_rope(qr, tables) * scale).astype(BF16)


def _row_call(body, x, pos, consts, small, outs, tm, name):
    m, d = x.shape
    resident = [pl.BlockSpec(a.shape, lambda i: (0, 0), pipeline_mode=pl.Buffered(1)) for a in (consts, *small)]
    return pl.pallas_call(
        body,
        grid=(m // tm,),
        in_specs=[pl.BlockSpec((tm, d), lambda i: (i, 0)), pl.BlockSpec((tm, 1), lambda i: (i, 0))] + resident,
        out_specs=[pl.BlockSpec((tm, w), lambda i: (i, 0)) for w in outs],
        out_shape=[jax.ShapeDtypeStruct((m, w), BF16) for w in outs],
        compiler_params=pltpu.CompilerParams(
            dimension_semantics=("parallel",), vmem_limit_bytes=VMEM_LIMIT),
        name=name,
    )(x, pos, consts, *small)


def _rope_consts():
    half = QK_ROPE_DIM // 2
    inv_freq = jnp.power(ROPE_THETA, -jnp.arange(0, QK_ROPE_DIM, 2, dtype=F32) / QK_ROPE_DIM)
    lane = np.arange(LANES)
    rows = jnp.zeros((8, LANES), F32)
    rows = rows.at[0, :QK_ROPE_DIM].set(jnp.concatenate([inv_freq, inv_freq]))
    rows = rows.at[1].set(jnp.asarray(lane < QK_ROPE_DIM, F32))
    rows = rows.at[2].set(jnp.asarray(-(lane < half).astype(np.float32)))
    rows = rows.at[3].set(jnp.asarray(((lane >= half) & (lane < QK_ROPE_DIM)).astype(np.float32)))
    return rows


def _pad_cols(a, width):
    return jnp.pad(a, [(0, 0)] * (a.ndim - 1) + [(0, width - a.shape[-1])])


def kernel(x, mem, positions, a_norm, a_w_in, a_w_out, kv_norm, w_dkv, g_ckv, w_ukv, g_k_nope, g_k_rope, b_norm, b_w_in, b_g_q_lat, b_w_uq, b_g_q_nope, b_g_q_rope, b_w_out, mem_norm, w_mem_kv, g_mem_q, g_mem_k):
    batch, seq, d = x.shape
    m = batch * seq
    mem_len = mem.shape[1]
    x2 = x.reshape(m, d)
    mem2 = mem.reshape(batch * mem_len, d)
    pos = positions.reshape(m, 1)
    consts = _rope_consts()
    row = lambda g: g.reshape(1, -1)

    def mem_kv(layer):
        w = w_mem_kv[layer].astype(BF16)
        mkv = _norm_matmul(mem2, mem_norm[layer], w, jnp.ones((2 * MEM_W,), F32), F32, 256, 512)
        return mkv.reshape(batch, mem_len, 2 * MEM_W)

    a_in_w = a_w_in.shape[-1]
    qscale = jnp.ones((a_in_w,), F32).at[:SB_W].set(HEAD_DIM ** -0.5)
    pa = _norm_matmul(x2, a_norm[0], a_w_in[0].astype(BF16), qscale, BF16, 1024, 512)
    sb = _sb_attention(pa, batch, seq, 256)
    qm_blk = 4 * SB_W // MEM_W
    x2 = _mem_out_proj(sb, pa, qm_blk, qm_blk + 1, mem_kv(0), g_mem_q[0], g_mem_k[0],
                       x2, a_w_out[0].astype(BF16), seq, 1024, 1024)

    wd = _pad_cols(w_dkv, KV_LORA_RANK + LANES).astype(BF16)
    kx, v = _row_call(
        _kv_kernel, x2, pos, consts,
        (row(kv_norm), wd, row(g_ckv), w_ukv.astype(BF16), row(g_k_nope), _pad_cols(row(g_k_rope), LANES)),
        (N_MLA_HEADS * QK_PAD, MLA_W), 512, "mla_kv_side")

    wuq = b_w_uq[0].reshape(Q_LORA_RANK, N_MLA_HEADS, QK_NOPE_DIM + QK_ROPE_DIM)
    wuq = _pad_cols(wuq, QK_PAD).reshape(Q_LORA_RANK, N_MLA_HEADS * QK_PAD).astype(BF16)
    qx, side = _row_call(
        _q_kernel, x2, pos, consts,
        (row(b_norm[0]), b_w_in[0].astype(BF16), row(b_g_q_lat[0]), wuq, row(b_g_q_nope[0]),
         _pad_cols(row(b_g_q_rope[0]), LANES)),
        (N_MLA_HEADS * QK_PAD, MLA_W + 2 * MEM_W), 256, "mla_q_side")
    att = _mla_attention(qx, kx, v, side, batch, seq, 256)
    qm_blk = MLA_W // MEM_W
    x2 = _mem_out_proj(att, side, qm_blk, qm_blk + 1, mem_kv(1), g_mem_q[1], g_mem_k[1],
                       x2, b_w_out[0].astype(BF16), seq, 1024, 1024)
    return x2.reshape(batch, seq, d)
```

```python
import functools
import math

import jax
import jax.numpy as jnp
import numpy as np
from jax import lax
from jax.experimental import pallas as pl
from jax.experimental.pallas import tpu as pltpu

F32 = jnp.float32
BF16 = jnp.bfloat16

HEAD_DIM = 128
N_SB_HEADS = 12
N_MEM_HEADS = 4
N_MLA_HEADS = 12
Q_LORA_RANK = 512
KV_LORA_RANK = 512
QK_NOPE_DIM = 128
QK_ROPE_DIM = 64
V_HEAD_DIM = 128
ROPE_THETA = 10000.0
EPS = 1e-6
SB_W = N_SB_HEADS * HEAD_DIM
MEM_W = N_MEM_HEADS * HEAD_DIM
MLA_W = N_MLA_HEADS * V_HEAD_DIM
LOG2_E = math.log2(math.e)
LANES = 128
QK_PAD = 2 * LANES
VMEM_LIMIT = 56 * 1024 * 1024

SB_TQ, SB_TK = 1024, 512
MLA_TQ, MLA_TK = 1024, 512
SB_SUB = 2 * LANES
EXP2_CLAMP = 126.0


def _rms_scale(x, width=None):
    ss = jnp.sum(x * x, axis=-1, keepdims=True)
    return lax.rsqrt(ss / (x.shape[-1] if width is None else width) + EPS)


def _silu(g):
    return g / (1.0 + jnp.exp(-g))


def _dot(a, b):
    return jnp.dot(a, b, preferred_element_type=F32)


def _dot_nt(a, b):
    return lax.dot_general(a, b, (((1,), (1,)), ((), ())), preferred_element_type=F32)


def _normmm_kernel(x_ref, g_ref, w_ref, cs_ref, o_ref, h_ref):
    @pl.when(pl.program_id(1) == 0)
    def _():
        x = x_ref[...]
        h_ref[...] = (x * _rms_scale(x) * g_ref[...]).astype(BF16)

    o_ref[...] = (_dot(h_ref[...], w_ref[...]) * cs_ref[...]).astype(o_ref.dtype)


def _norm_matmul(x, g, w, colscale, out_dtype, tm, tn):
    m, k = x.shape
    n = w.shape[1]
    return pl.pallas_call(
        _normmm_kernel,
        grid=(m // tm, n // tn),
        in_specs=[
            pl.BlockSpec((tm, k), lambda i, j: (i, 0)),
            pl.BlockSpec((1, k), lambda i, j: (0, 0)),
            pl.BlockSpec((k, tn), lambda i, j: (0, j)),
            pl.BlockSpec((1, tn), lambda i, j: (0, j)),
        ],
        out_specs=pl.BlockSpec((tm, tn), lambda i, j: (i, j)),
        out_shape=jax.ShapeDtypeStruct((m, n), out_dtype),
        scratch_shapes=[pltpu.VMEM((tm, k), BF16)],
        compiler_params=pltpu.CompilerParams(
            dimension_semantics=("parallel", "arbitrary"), vmem_limit_bytes=VMEM_LIMIT),
        name="norm_matmul",
    )(x, g.reshape(1, k), w, colscale.reshape(1, n))


def _mask_top(x, keep, fill):
    n = keep.shape[0]
    top = jnp.where(keep, x[:n], fill)
    return top if x.shape[0] == n else jnp.concatenate([top, x[n:]], axis=0)


def _walk_key_blocks(block, i, nd):
    for d in reversed(range(nd)):
        block(i * nd + d, d, True)

    def body(j, carry):
        block(i * nd - 1 - j, 0, False)
        return carry

    lax.fori_loop(0, i * nd, body, 0)


def _sb_kernel(q_ref, k_ref, v_ref, g_ref, u_ref, o_ref, acc_ref, carry_ref, *, tq, tk):
    i = pl.program_id(2)
    u = u_ref[...]
    row = lax.broadcasted_iota(jnp.int32, (tk, tk), 0)
    col = lax.broadcasted_iota(jnp.int32, (tk, tk), 1)
    causal = col < row
    acc_ref[...] = jnp.zeros_like(acc_ref)
    carry_ref[...] = jnp.zeros_like(carry_ref)

    def block(kb, d, diagonal):
        r0 = d * tk
        start = pl.multiple_of(kb * tk, tk)
        z = _dot_nt(q_ref[r0:, :], k_ref[pl.ds(start, tk), :])
        neg_log = jnp.maximum(z, jnp.log(1.0 + jnp.exp2(jnp.minimum(z, EXP2_CLAMP))) * LOG2_E)
        if diagonal:
            neg_log = _mask_top(neg_log, causal, 0.0)
        hi = neg_log.astype(BF16)
        seen = carry_ref[r0:, :]
        pieces = []
        for c in reversed(range(tk // SB_SUB)):
            sub = slice(c * SB_SUB, (c + 1) * SB_SUB)
            suffix = _dot(hi[:, sub], u)
            newer = jnp.concatenate([seen] * (SB_SUB // LANES), axis=1)
            pieces.append(jnp.exp2(z[:, sub] - suffix - newer))
            seen = seen + jnp.sum(neg_log[:, sub], axis=-1, keepdims=True)
        a = jnp.concatenate(pieces[::-1], axis=1)
        if diagonal:
            a = _mask_top(a, causal, 0.0)
        acc_ref[r0:, :] += _dot(a.astype(BF16), v_ref[pl.ds(start, tk), :])
        carry_ref[r0:, :] = seen

    _walk_key_blocks(block, i, tq // tk)
    o_ref[...] = (acc_ref[...] * _silu(g_ref[...].astype(F32))).astype(o_ref.dtype)


def _sb_attention(pa, batch, seq):
    tq, tk = SB_TQ, SB_TK
    nq = seq // tq
    h0 = N_SB_HEADS
    u = jnp.tril(jnp.ones((SB_SUB, SB_SUB), F32)).astype(BF16)
    return pl.pallas_call(
        functools.partial(_sb_kernel, tq=tq, tk=tk),
        grid=(batch, N_SB_HEADS, nq),
        in_specs=[
            pl.BlockSpec((tq, HEAD_DIM), lambda b, h, i: (b * nq + i, h)),
            pl.BlockSpec((seq, HEAD_DIM), lambda b, h, i: (b, h0 + h)),
            pl.BlockSpec((seq, HEAD_DIM), lambda b, h, i: (b, 2 * h0 + h)),
            pl.BlockSpec((tq, HEAD_DIM), lambda b, h, i: (b * nq + i, 3 * h0 + h)),
            pl.BlockSpec((SB_SUB, SB_SUB), lambda b, h, i: (0, 0)),
        ],
        out_specs=pl.BlockSpec((tq, HEAD_DIM), lambda b, h, i: (b * nq + i, h)),
        out_shape=jax.ShapeDtypeStruct((batch * seq, SB_W), BF16),
        scratch_shapes=[pltpu.VMEM((tq, HEAD_DIM), F32), pltpu.VMEM((tq, LANES), F32)],
        compiler_params=pltpu.CompilerParams(
            dimension_semantics=("parallel", "parallel", "arbitrary"), vmem_limit_bytes=VMEM_LIMIT),
        name="sb_attention",
    )(pa, pa, pa, pa, u)


def _mla_kernel(q_ref, k_ref, v_ref, g_ref, o_ref, acc_ref, m_ref, den_ref, *, tq, tk):
    i = pl.program_id(2)
    row = lax.broadcasted_iota(jnp.int32, (tk, tk), 0)
    col = lax.broadcasted_iota(jnp.int32, (tk, tk), 1)
    causal = col <= row
    acc_ref[...] = jnp.zeros_like(acc_ref)
    den_ref[...] = jnp.zeros_like(den_ref)
    m_ref[...] = jnp.full_like(m_ref, -jnp.inf)

    def block(kb, d, diagonal):
        r0 = d * tk
        start = pl.multiple_of(kb * tk, tk)
        s = _dot_nt(q_ref[r0:, :], k_ref[pl.ds(start, tk), :])
        if diagonal:
            s = _mask_top(s, causal, -jnp.inf)
        m_old = m_ref[r0:, :]
        m_new = jnp.maximum(m_old, jnp.max(s, axis=-1, keepdims=True))
        alpha = jnp.exp2(m_old - m_new)
        p = jnp.concatenate(
            [jnp.exp2(s[:, c * LANES:(c + 1) * LANES] - m_new) for c in range(tk // LANES)], axis=1)
        den_ref[r0:, :] = alpha * den_ref[r0:, :] + jnp.sum(p, axis=-1, keepdims=True)
        acc_ref[r0:, :] = alpha * acc_ref[r0:, :] + _dot(p.astype(BF16), v_ref[pl.ds(start, tk), :])
        m_ref[r0:, :] = m_new

    _walk_key_blocks(block, i, tq // tk)
    o_ref[...] = (acc_ref[...] / den_ref[...] * _silu(g_ref[...].astype(F32))).astype(o_ref.dtype)


def _mla_attention(qx, kx, v, gq, batch, seq):
    tq, tk = MLA_TQ, MLA_TK
    nq = seq // tq
    return pl.pallas_call(
        functools.partial(_mla_kernel, tq=tq, tk=tk),
        grid=(batch, N_MLA_HEADS, nq),
        in_specs=[
            pl.BlockSpec((tq, QK_PAD), lambda b, h, i: (b * nq + i, h)),
            pl.BlockSpec((seq, QK_PAD), lambda b, h, i: (b, h)),
            pl.BlockSpec((seq, V_HEAD_DIM), lambda b, h, i: (b, h)),
            pl.BlockSpec((tq, V_HEAD_DIM), lambda b, h, i: (b * nq + i, h)),
        ],
        out_specs=pl.BlockSpec((tq, V_HEAD_DIM), lambda b, h, i: (b * nq + i, h)),
        out_shape=jax.ShapeDtypeStruct((batch * seq, MLA_W), BF16),
        scratch_shapes=[pltpu.VMEM((tq, V_HEAD_DIM), F32), pltpu.VMEM((tq, LANES), F32),
                        pltpu.VMEM((tq, LANES), F32)],
        compiler_params=pltpu.CompilerParams(
            dimension_semantics=("parallel", "parallel", "arbitrary"), vmem_limit_bytes=VMEM_LIMIT),
        name="mla_attention",
    )(qx, kx, v, gq)


def _out_kernel(main_ref, qm_ref, gm_ref, mkv_ref, gq_ref, gk_ref, x_ref, w_ref, o_ref, mo_ref, *, main_w):
    @pl.when(pl.program_id(1) == 0)
    def _():
        for h in range(N_MEM_HEADS):
            lo, hi = h * HEAD_DIM, (h + 1) * HEAD_DIM
            mk = mkv_ref[0, :, lo:hi]
            mk = (mk * _rms_scale(mk) * gk_ref[...]).astype(BF16)
            mv = mkv_ref[0, :, MEM_W + lo:MEM_W + hi].astype(BF16)
            q = qm_ref[:, lo:hi].astype(F32)
            q = (q * _rms_scale(q) * gq_ref[...]).astype(BF16)
            s = _dot_nt(q, mk) * HEAD_DIM ** -0.5
            p = jnp.exp(s - jnp.max(s, axis=-1, keepdims=True))
            mo = _dot(p.astype(BF16), mv) / jnp.sum(p, axis=-1, keepdims=True)
            mo_ref[:, lo:hi] = (mo * _silu(gm_ref[:, lo:hi].astype(F32))).astype(BF16)

    acc = _dot(main_ref[...], w_ref[:main_w, :]) + _dot(mo_ref[...], w_ref[main_w:, :])
    o_ref[...] = x_ref[...] + acc


def _mem_out_proj(main, side, qm_blk, gm_blk, mkv, g_q, g_k, x, w, seq, tm, tn):
    m, main_w = main.shape
    d = w.shape[1]
    mem_len = mkv.shape[1]
    per_b = seq // tm
    return pl.pallas_call(
        functools.partial(_out_kernel, main_w=main_w),
        grid=(m // tm, d // tn),
        in_specs=[
            pl.BlockSpec((tm, main_w), lambda i, j: (i, 0)),
            pl.BlockSpec((tm, MEM_W), lambda i, j: (i, qm_blk)),
            pl.BlockSpec((tm, MEM_W), lambda i, j: (i, gm_blk)),
            pl.BlockSpec((1, mem_len, 2 * MEM_W), lambda i, j: (i // per_b, 0, 0)),
            pl.BlockSpec((1, HEAD_DIM), lambda i, j: (0, 0)),
            pl.BlockSpec((1, HEAD_DIM), lambda i, j: (0, 0)),
            pl.BlockSpec((tm, tn), lambda i, j: (i, j)),
            pl.BlockSpec((main_w + MEM_W, tn), lambda i, j: (0, j)),
        ],
        out_specs=pl.BlockSpec((tm, tn), lambda i, j: (i, j)),
        out_shape=jax.ShapeDtypeStruct((m, d), F32),
        scratch_shapes=[pltpu.VMEM((tm, MEM_W), BF16)],
        compiler_params=pltpu.CompilerParams(
            dimension_semantics=("parallel", "arbitrary"), vmem_limit_bytes=VMEM_LIMIT),
        name="mem_out_proj",
    )(main, side, side, mkv, g_q.reshape(1, HEAD_DIM), g_k.reshape(1, HEAD_DIM), x, w)


def _rope_tables(pos_ref, c_ref):
    ang = pos_ref[...].astype(F32) * c_ref[0:1, :]
    cosv, sinv = jnp.cos(ang), jnp.sin(ang)
    return cosv * c_ref[1:2, :], sinv * c_ref[2:3, :], sinv * c_ref[3:4, :]


def _rope(x, tables):
    c, s1, s2 = tables
    return x * c + pltpu.roll(x, LANES - QK_ROPE_DIM // 2, 1) * s1 + pltpu.roll(x, QK_ROPE_DIM // 2, 1) * s2


def _kv_kernel(x_ref, pos_ref, c_ref, gx_ref, wd_ref, gc_ref, wu_ref, gkn_ref, gkr_ref, kx_ref, v_ref):
    x = x_ref[...]
    h = (x * _rms_scale(x) * gx_ref[...]).astype(BF16)
    c = _dot(h, wd_ref[...])
    ckv = c[:, :KV_LORA_RANK]
    cn = (ckv * _rms_scale(ckv) * gc_ref[...]).astype(BF16)
    kv = _dot(cn, wu_ref[...])
    kr = c[:, KV_LORA_RANK:]
    kr = kr * _rms_scale(kr, QK_ROPE_DIM) * gkr_ref[...]
    k_rope = _rope(kr, _rope_tables(pos_ref, c_ref)).astype(BF16)
    for hd in range(N_MLA_HEADS):
        base = hd * QK_PAD
        kn = kv[:, base:base + QK_NOPE_DIM]
        kx_ref[:, base:base + QK_NOPE_DIM] = (kn * _rms_scale(kn) * gkn_ref[...]).astype(BF16)
        kx_ref[:, base + QK_NOPE_DIM:base + QK_PAD] = k_rope
        v_ref[:, hd * V_HEAD_DIM:(hd + 1) * V_HEAD_DIM] = kv[:, base + QK_NOPE_DIM:base + QK_PAD].astype(BF16)


def _q_kernel(x_ref, pos_ref, c_ref, gx_ref, win_ref, gl_ref, wuq_ref, gqn_ref, gqr_ref, qx_ref, side_ref):
    x = x_ref[...]
    h = (x * _rms_scale(x) * gx_ref[...]).astype(BF16)
    p = _dot(h, win_ref[...])
    side_ref[...] = p[:, Q_LORA_RANK:].astype(BF16)
    ql = p[:, :Q_LORA_RANK]
    qn = (ql * _rms_scale(ql) * gl_ref[...]).astype(BF16)
    q = _dot(qn, wuq_ref[...])
    tables = _rope_tables(pos_ref, c_ref)
    scale = (QK_NOPE_DIM + QK_ROPE_DIM) ** -0.5 * LOG2_E
    for hd in range(N_MLA_HEADS):
        base = hd * QK_PAD
        qnope = q[:, base:base + QK_NOPE_DIM]
        qx_ref[:, base:base + QK_NOPE_DIM] = (qnope * _rms_scale(qnope) * gqn_ref[...] * scale).astype(BF16)
        qr = q[:, base + QK_NOPE_DIM:base + QK_PAD]
        qr = qr * _rms_scale(qr, QK_ROPE_DIM) * gqr_ref[...]
        qx_ref[:, base + QK_NOPE_DIM:base + QK_PAD] = (_rope(qr, tables) * scale).astype(BF16)


def _row_call(body, x, pos, consts, small, outs, tm, name):
    m, d = x.shape
    resident = [pl.BlockSpec(a.shape, lambda i: (0, 0), pipeline_mode=pl.Buffered(1)) for a in (consts, *small)]
    return pl.pallas_call(
        body,
        grid=(m // tm,),
        in_specs=[pl.BlockSpec((tm, d), lambda i: (i, 0)), pl.BlockSpec((tm, 1), lambda i: (i, 0))] + resident,
        out_specs=[pl.BlockSpec((tm, w), lambda i: (i, 0)) for w in outs],
        out_shape=[jax.ShapeDtypeStruct((m, w), BF16) for w in outs],
        compiler_params=pltpu.CompilerParams(
            dimension_semantics=("parallel",), vmem_limit_bytes=VMEM_LIMIT),
        name=name,
    )(x, pos, consts, *small)


def _rope_consts():
    half = QK_ROPE_DIM // 2
    inv_freq = jnp.power(ROPE_THETA, -jnp.arange(0, QK_ROPE_DIM, 2, dtype=F32) / QK_ROPE_DIM)
    lane = np.arange(LANES)
    rows = jnp.zeros((8, LANES), F32)
    rows = rows.at[0, :QK_ROPE_DIM].set(jnp.concatenate([inv_freq, inv_freq]))
    rows = rows.at[1].set(jnp.asarray(lane < QK_ROPE_DIM, F32))
    rows = rows.at[2].set(jnp.asarray(-(lane < half).astype(np.float32)))
    rows = rows.at[3].set(jnp.asarray(((lane >= half) & (lane < QK_ROPE_DIM)).astype(np.float32)))
    return rows


def _pad_cols(a, width):
    return jnp.pad(a, [(0, 0)] * (a.ndim - 1) + [(0, width - a.shape[-1])])


def kernel(x, mem, positions, a_norm, a_w_in, a_w_out, kv_norm, w_dkv, g_ckv, w_ukv, g_k_nope, g_k_rope, b_norm, b_w_in, b_g_q_lat, b_w_uq, b_g_q_nope, b_g_q_rope, b_w_out, mem_norm, w_mem_kv, g_mem_q, g_mem_k):
    batch, seq, d = x.shape
    m = batch * seq
    mem_len = mem.shape[1]
    x2 = x.reshape(m, d)
    mem2 = mem.reshape(batch * mem_len, d)
    pos = positions.reshape(m, 1)
    consts = _rope_consts()
    row = lambda g: g.reshape(1, -1)

    def mem_kv(layer):
        w = w_mem_kv[layer].astype(BF16)
        mkv = _norm_matmul(mem2, mem_norm[layer], w, jnp.ones((2 * MEM_W,), F32), F32, 256, 512)
        return mkv.reshape(batch, mem_len, 2 * MEM_W)

    a_in_w = a_w_in.shape[-1]
    qscale = jnp.ones((a_in_w,), F32).at[:SB_W].set(HEAD_DIM ** -0.5 * LOG2_E)
    pa = _norm_matmul(x2, a_norm[0], a_w_in[0].astype(BF16), qscale, BF16, 1024, 512)
    sb = _sb_attention(pa, batch, seq)
    qm_blk = 4 * SB_W // MEM_W
    x2 = _mem_out_proj(sb, pa, qm_blk, qm_blk + 1, mem_kv(0), g_mem_q[0], g_mem_k[0],
                       x2, a_w_out[0].astype(BF16), seq, 1024, 1024)

    wd = _pad_cols(w_dkv, KV_LORA_RANK + LANES).astype(BF16)
    kx, v = _row_call(
        _kv_kernel, x2, pos, consts,
        (row(kv_norm), wd, row(g_ckv), w_ukv.astype(BF16), row(g_k_nope), _pad_cols(row(g_k_rope), LANES)),
        (N_MLA_HEADS * QK_PAD, MLA_W), 512, "mla_kv_side")

    wuq = b_w_uq[0].reshape(Q_LORA_RANK, N_MLA_HEADS, QK_NOPE_DIM + QK_ROPE_DIM)
    wuq = _pad_cols(wuq, QK_PAD).reshape(Q_LORA_RANK, N_MLA_HEADS * QK_PAD).astype(BF16)
    qx, side = _row_call(
        _q_kernel, x2, pos, consts,
        (row(b_norm[0]), b_w_in[0].astype(BF16), row(b_g_q_lat[0]), wuq, row(b_g_q_nope[0]),
         _pad_cols(row(b_g_q_rope[0]), LANES)),
        (N_MLA_HEADS * QK_PAD, MLA_W + 2 * MEM_W), 256, "mla_q_side")
    att = _mla_attention(qx, kx, v, side, batch, seq)
    qm_blk = MLA_W // MEM_W
    x2 = _mem_out_proj(att, side, qm_blk, qm_blk + 1, mem_kv(1), g_mem_q[1], g_mem_k[1],
                       x2, b_w_out[0].astype(BF16), seq, 1024, 1024)
    return x2.reshape(batch, seq, d)
```

```python
import functools
import math

import jax
import jax.numpy as jnp
import numpy as np
from jax import lax
from jax.experimental import pallas as pl
from jax.experimental.pallas import tpu as pltpu

F32 = jnp.float32
BF16 = jnp.bfloat16

HEAD_DIM = 128
N_SB_HEADS = 12
N_MEM_HEADS = 4
N_MLA_HEADS = 12
Q_LORA_RANK = 512
KV_LORA_RANK = 512
QK_NOPE_DIM = 128
QK_ROPE_DIM = 64
V_HEAD_DIM = 128
ROPE_THETA = 10000.0
EPS = 1e-6
SB_W = N_SB_HEADS * HEAD_DIM
MEM_W = N_MEM_HEADS * HEAD_DIM
MLA_W = N_MLA_HEADS * V_HEAD_DIM
LOG2_E = math.log2(math.e)
LANES = 128
QK_PAD = 2 * LANES
VMEM_LIMIT = 56 * 1024 * 1024

SB_TQ, SB_TK = 1024, 256
MLA_TQ, MLA_TK = 1024, 512
SB_SUB = 2 * LANES
EXP2_CLAMP = 126.0
SB_DEAD_LOG2 = 150.0


def _rms_scale(x, width=None):
    ss = jnp.sum(x * x, axis=-1, keepdims=True)
    return lax.rsqrt(ss / (x.shape[-1] if width is None else width) + EPS)


def _silu(g):
    return g / (1.0 + jnp.exp(-g))


def _dot(a, b):
    return jnp.dot(a, b, preferred_element_type=F32)


def _dot_nt(a, b):
    return lax.dot_general(a, b, (((1,), (1,)), ((), ())), preferred_element_type=F32)


def _normmm_kernel(x_ref, g_ref, w_ref, cs_ref, o_ref, h_ref):
    @pl.when(pl.program_id(1) == 0)
    def _():
        x = x_ref[...]
        h_ref[...] = (x * _rms_scale(x) * g_ref[...]).astype(BF16)

    o_ref[...] = (_dot(h_ref[...], w_ref[...]) * cs_ref[...]).astype(o_ref.dtype)


def _norm_matmul(x, g, w, colscale, out_dtype, tm, tn):
    m, k = x.shape
    n = w.shape[1]
    return pl.pallas_call(
        _normmm_kernel,
        grid=(m // tm, n // tn),
        in_specs=[
            pl.BlockSpec((tm, k), lambda i, j: (i, 0)),
            pl.BlockSpec((1, k), lambda i, j: (0, 0)),
            pl.BlockSpec((k, tn), lambda i, j: (0, j)),
            pl.BlockSpec((1, tn), lambda i, j: (0, j)),
        ],
        out_specs=pl.BlockSpec((tm, tn), lambda i, j: (i, j)),
        out_shape=jax.ShapeDtypeStruct((m, n), out_dtype),
        scratch_shapes=[pltpu.VMEM((tm, k), BF16)],
        compiler_params=pltpu.CompilerParams(
            dimension_semantics=("parallel", "arbitrary"), vmem_limit_bytes=VMEM_LIMIT),
        name="norm_matmul",
    )(x, g.reshape(1, k), w, colscale.reshape(1, n))


def _mask_top(x, keep, fill):
    n = keep.shape[0]
    top = jnp.where(keep, x[:n], fill)
    return top if x.shape[0] == n else jnp.concatenate([top, x[n:]], axis=0)


def _walk_key_blocks(block, i, nd, alive=None):
    for d in reversed(range(nd)):
        block(i * nd + d, d, True)

    if alive is None:
        def body(j, carry):
            block(i * nd - 1 - j, 0, False)
            return carry

        lax.fori_loop(0, i * nd, body, 0)
    else:
        def step(state):
            block(i * nd - 1 - state[0], 0, False)
            return state[0] + 1, alive()

        lax.while_loop(lambda state: jnp.logical_and(state[0] < i * nd, state[1]), step, (0, alive()))


def _sb_kernel(q_ref, k_ref, v_ref, g_ref, u_ref, o_ref, acc_ref, carry_ref, *, tq, tk):
    i = pl.program_id(2)
    u = u_ref[...]
    row = lax.broadcasted_iota(jnp.int32, (tk, tk), 0)
    col = lax.broadcasted_iota(jnp.int32, (tk, tk), 1)
    causal = col < row
    acc_ref[...] = jnp.zeros_like(acc_ref)
    carry_ref[...] = jnp.zeros_like(carry_ref)

    def block(kb, d, diagonal):
        r0 = d * tk
        start = pl.multiple_of(kb * tk, tk)
        z = _dot_nt(q_ref[r0:, :], k_ref[pl.ds(start, tk), :])
        neg_log = jnp.maximum(z, jnp.log(1.0 + jnp.exp2(jnp.minimum(z, EXP2_CLAMP))) * LOG2_E)
        if diagonal:
            neg_log = _mask_top(neg_log, causal, 0.0)
        hi = neg_log.astype(BF16)
        seen = carry_ref[r0:, :]
        pieces = []
        for c in reversed(range(tk // SB_SUB)):
            sub = slice(c * SB_SUB, (c + 1) * SB_SUB)
            suffix = _dot(hi[:, sub], u)
            newer = jnp.concatenate([seen] * (SB_SUB // LANES), axis=1)
            pieces.append(jnp.exp2(z[:, sub] - suffix - newer))
            seen = seen + jnp.sum(neg_log[:, sub], axis=-1, keepdims=True)
        a = jnp.concatenate(pieces[::-1], axis=1)
        if diagonal:
            a = _mask_top(a, causal, 0.0)
        acc_ref[r0:, :] += _dot(a.astype(BF16), v_ref[pl.ds(start, tk), :])
        carry_ref[r0:, :] = seen

    _walk_key_blocks(block, i, tq // tk, alive=lambda: jnp.min(carry_ref[...]) < SB_DEAD_LOG2)
    o_ref[...] = (acc_ref[...] * _silu(g_ref[...].astype(F32))).astype(o_ref.dtype)


def _sb_attention(pa, batch, seq):
    tq, tk = SB_TQ, SB_TK
    nq = seq // tq
    h0 = N_SB_HEADS
    u = jnp.tril(jnp.ones((SB_SUB, SB_SUB), F32)).astype(BF16)
    return pl.pallas_call(
        functools.partial(_sb_kernel, tq=tq, tk=tk),
        grid=(batch, N_SB_HEADS, nq),
        in_specs=[
            pl.BlockSpec((tq, HEAD_DIM), lambda b, h, i: (b * nq + i, h)),
            pl.BlockSpec((seq, HEAD_DIM), lambda b, h, i: (b, h0 + h)),
            pl.BlockSpec((seq, HEAD_DIM), lambda b, h, i: (b, 2 * h0 + h)),
            pl.BlockSpec((tq, HEAD_DIM), lambda b, h, i: (b * nq + i, 3 * h0 + h)),
            pl.BlockSpec((SB_SUB, SB_SUB), lambda b, h, i: (0, 0)),
        ],
        out_specs=pl.BlockSpec((tq, HEAD_DIM), lambda b, h, i: (b * nq + i, h)),
        out_shape=jax.ShapeDtypeStruct((batch * seq, SB_W), BF16),
        scratch_shapes=[pltpu.VMEM((tq, HEAD_DIM), F32), pltpu.VMEM((tq, LANES), F32)],
        compiler_params=pltpu.CompilerParams(
            dimension_semantics=("parallel", "parallel", "arbitrary"), vmem_limit_bytes=VMEM_LIMIT),
        name="sb_attention",
    )(pa, pa, pa, pa, u)


def _mla_kernel(q_ref, k_ref, v_ref, g_ref, o_ref, acc_ref, m_ref, den_ref, *, tq, tk):
    i = pl.program_id(2)
    row = lax.broadcasted_iota(jnp.int32, (tk, tk), 0)
    col = lax.broadcasted_iota(jnp.int32, (tk, tk), 1)
    causal = col <= row
    acc_ref[...] = jnp.zeros_like(acc_ref)
    den_ref[...] = jnp.zeros_like(den_ref)
    m_ref[...] = jnp.full_like(m_ref, -jnp.inf)

    def block(kb, d, diagonal):
        r0 = d * tk
        start = pl.multiple_of(kb * tk, tk)
        s = _dot_nt(q_ref[r0:, :], k_ref[pl.ds(start, tk), :])
        if diagonal:
            s = _mask_top(s, causal, -jnp.inf)
        m_old = m_ref[r0:, :]
        m_new = jnp.maximum(m_old, jnp.max(s, axis=-1, keepdims=True))
        alpha = jnp.exp2(m_old - m_new)
        p = jnp.concatenate(
            [jnp.exp2(s[:, c * LANES:(c + 1) * LANES] - m_new) for c in range(tk // LANES)], axis=1)
        den_ref[r0:, :] = alpha * den_ref[r0:, :] + jnp.sum(p, axis=-1, keepdims=True)
        acc_ref[r0:, :] = alpha * acc_ref[r0:, :] + _dot(p.astype(BF16), v_ref[pl.ds(start, tk), :])
        m_ref[r0:, :] = m_new

    _walk_key_blocks(block, i, tq // tk)
    o_ref[...] = (acc_ref[...] / den_ref[...] * _silu(g_ref[...].astype(F32))).astype(o_ref.dtype)


def _mla_attention(qx, kx, v, gq, batch, seq):
    tq, tk = MLA_TQ, MLA_TK
    nq = seq // tq
    return pl.pallas_call(
        functools.partial(_mla_kernel, tq=tq, tk=tk),
        grid=(batch, N_MLA_HEADS, nq),
        in_specs=[
            pl.BlockSpec((tq, QK_PAD), lambda b, h, i: (b * nq + i, h)),
            pl.BlockSpec((seq, QK_PAD), lambda b, h, i: (b, h)),
            pl.BlockSpec((seq, V_HEAD_DIM), lambda b, h, i: (b, h)),
            pl.BlockSpec((tq, V_HEAD_DIM), lambda b, h, i: (b * nq + i, h)),
        ],
        out_specs=pl.BlockSpec((tq, V_HEAD_DIM), lambda b, h, i: (b * nq + i, h)),
        out_shape=jax.ShapeDtypeStruct((batch * seq, MLA_W), BF16),
        scratch_shapes=[pltpu.VMEM((tq, V_HEAD_DIM), F32), pltpu.VMEM((tq, LANES), F32),
                        pltpu.VMEM((tq, LANES), F32)],
        compiler_params=pltpu.CompilerParams(
            dimension_semantics=("parallel", "parallel", "arbitrary"), vmem_limit_bytes=VMEM_LIMIT),
        name="mla_attention",
    )(qx, kx, v, gq)


def _out_kernel(main_ref, qm_ref, gm_ref, mkv_ref, gq_ref, gk_ref, x_ref, w_ref, o_ref, mo_ref, *, main_w):
    @pl.when(pl.program_id(1) == 0)
    def _():
        for h in range(N_MEM_HEADS):
            lo, hi = h * HEAD_DIM, (h + 1) * HEAD_DIM
            mk = mkv_ref[0, :, lo:hi]
            mk = (mk * _rms_scale(mk) * gk_ref[...]).astype(BF16)
            mv = mkv_ref[0, :, MEM_W + lo:MEM_W + hi].astype(BF16)
            q = qm_ref[:, lo:hi].astype(F32)
            q = (q * _rms_scale(q) * gq_ref[...]).astype(BF16)
            s = _dot_nt(q, mk) * HEAD_DIM ** -0.5
            p = jnp.exp(s - jnp.max(s, axis=-1, keepdims=True))
            mo = _dot(p.astype(BF16), mv) / jnp.sum(p, axis=-1, keepdims=True)
            mo_ref[:, lo:hi] = (mo * _silu(gm_ref[:, lo:hi].astype(F32))).astype(BF16)

    acc = _dot(main_ref[...], w_ref[:main_w, :]) + _dot(mo_ref[...], w_ref[main_w:, :])
    o_ref[...] = x_ref[...] + acc


def _mem_out_proj(main, side, qm_blk, gm_blk, mkv, g_q, g_k, x, w, seq, tm, tn):
    m, main_w = main.shape
    d = w.shape[1]
    mem_len = mkv.shape[1]
    per_b = seq // tm
    return pl.pallas_call(
        functools.partial(_out_kernel, main_w=main_w),
        grid=(m // tm, d // tn),
        in_specs=[
            pl.BlockSpec((tm, main_w), lambda i, j: (i, 0)),
            pl.BlockSpec((tm, MEM_W), lambda i, j: (i, qm_blk)),
            pl.BlockSpec((tm, MEM_W), lambda i, j: (i, gm_blk)),
            pl.BlockSpec((1, mem_len, 2 * MEM_W), lambda i, j: (i // per_b, 0, 0)),
            pl.BlockSpec((1, HEAD_DIM), lambda i, j: (0, 0)),
            pl.BlockSpec((1, HEAD_DIM), lambda i, j: (0, 0)),
            pl.BlockSpec((tm, tn), lambda i, j: (i, j)),
            pl.BlockSpec((main_w + MEM_W, tn), lambda i, j: (0, j)),
        ],
        out_specs=pl.BlockSpec((tm, tn), lambda i, j: (i, j)),
        out_shape=jax.ShapeDtypeStruct((m, d), F32),
        scratch_shapes=[pltpu.VMEM((tm, MEM_W), BF16)],
        compiler_params=pltpu.CompilerParams(
            dimension_semantics=("parallel", "arbitrary"), vmem_limit_bytes=VMEM_LIMIT),
        name="mem_out_proj",
    )(main, side, side, mkv, g_q.reshape(1, HEAD_DIM), g_k.reshape(1, HEAD_DIM), x, w)


def _rope_tables(pos_ref, c_ref):
    ang = pos_ref[...].astype(F32) * c_ref[0:1, :]
    cosv, sinv = jnp.cos(ang), jnp.sin(ang)
    return cosv * c_ref[1:2, :], sinv * c_ref[2:3, :], sinv * c_ref[3:4, :]


def _rope(x, tables):
    c, s1, s2 = tables
    return x * c + pltpu.roll(x, LANES - QK_ROPE_DIM // 2, 1) * s1 + pltpu.roll(x, QK_ROPE_DIM // 2, 1) * s2


def _kv_kernel(x_ref, pos_ref, c_ref, gx_ref, wd_ref, gc_ref, wu_ref, gkn_ref, gkr_ref, kx_ref, v_ref):
    x = x_ref[...]
    h = (x * _rms_scale(x) * gx_ref[...]).astype(BF16)
    c = _dot(h, wd_ref[...])
    ckv = c[:, :KV_LORA_RANK]
    cn = (ckv * _rms_scale(ckv) * gc_ref[...]).astype(BF16)
    kv = _dot(cn, wu_ref[...])
    kr = c[:, KV_LORA_RANK:]
    kr = kr * _rms_scale(kr, QK_ROPE_DIM) * gkr_ref[...]
    k_rope = _rope(kr, _rope_tables(pos_ref, c_ref)).astype(BF16)
    for hd in range(N_MLA_HEADS):
        base = hd * QK_PAD
        kn = kv[:, base:base + QK_NOPE_DIM]
        kx_ref[:, base:base + QK_NOPE_DIM] = (kn * _rms_scale(kn) * gkn_ref[...]).astype(BF16)
        kx_ref[:, base + QK_NOPE_DIM:base + QK_PAD] = k_rope
        v_ref[:, hd * V_HEAD_DIM:(hd + 1) * V_HEAD_DIM] = kv[:, base + QK_NOPE_DIM:base + QK_PAD].astype(BF16)


def _q_kernel(x_ref, pos_ref, c_ref, gx_ref, win_ref, gl_ref, wuq_ref, gqn_ref, gqr_ref, qx_ref, side_ref):
    x = x_ref[...]
    h = (x * _rms_scale(x) * gx_ref[...]).astype(BF16)
    p = _dot(h, win_ref[...])
    side_ref[...] = p[:, Q_LORA_RANK:].astype(BF16)
    ql = p[:, :Q_LORA_RANK]
    qn = (ql * _rms_scale(ql) * gl_ref[...]).astype(BF16)
    q = _dot(qn, wuq_ref[...])
    tables = _rope_tables(pos_ref, c_ref)
    scale = (QK_NOPE_DIM + QK_ROPE_DIM) ** -0.5 * LOG2_E
    for hd in range(N_MLA_HEADS):
        base = hd * QK_PAD
        qnope = q[:, base:base + QK_NOPE_DIM]
        qx_ref[:, base:base + QK_NOPE_DIM] = (qnope * _rms_scale(qnope) * gqn_ref[...] * scale).astype(BF16)
        qr = q[:, base + QK_NOPE_DIM:base + QK_PAD]
        qr = qr * _rms_scale(qr, QK_ROPE_DIM) * gqr_ref[...]
        qx_ref[:, base + QK_NOPE_DIM:base + QK_PAD] = (_rope(qr, tables) * scale).astype(BF16)


def _row_call(body, x, pos, consts, small, outs, tm, name):
    m, d = x.shape
    resident = [pl.BlockSpec(a.shape, lambda i: (0, 0), pipeline_mode=pl.Buffered(1)) for a in (consts, *small)]
    return pl.pallas_call(
        body,
        grid=(m // tm,),
        in_specs=[pl.BlockSpec((tm, d), lambda i: (i, 0)), pl.BlockSpec((tm, 1), lambda i: (i, 0))] + resident,
        out_specs=[pl.BlockSpec((tm, w), lambda i: (i, 0)) for w in outs],
        out_shape=[jax.ShapeDtypeStruct((m, w), BF16) for w in outs],
        compiler_params=pltpu.CompilerParams(
            dimension_semantics=("parallel",), vmem_limit_bytes=VMEM_LIMIT),
        name=name,
    )(x, pos, consts, *small)


def _rope_consts():
    half = QK_ROPE_DIM // 2
    inv_freq = jnp.power(ROPE_THETA, -jnp.arange(0, QK_ROPE_DIM, 2, dtype=F32) / QK_ROPE_DIM)
    lane = np.arange(LANES)
    rows = jnp.zeros((8, LANES), F32)
    rows = rows.at[0, :QK_ROPE_DIM].set(jnp.concatenate([inv_freq, inv_freq]))
    rows = rows.at[1].set(jnp.asarray(lane < QK_ROPE_DIM, F32))
    rows = rows.at[2].set(jnp.asarray(-(lane < half).astype(np.float32)))
    rows = rows.at[3].set(jnp.asarray(((lane >= half) & (lane < QK_ROPE_DIM)).astype(np.float32)))
    return rows


def _pad_cols(a, width):
    return jnp.pad(a, [(0, 0)] * (a.ndim - 1) + [(0, width - a.shape[-1])])


def kernel(x, mem, positions, a_norm, a_w_in, a_w_out, kv_norm, w_dkv, g_ckv, w_ukv, g_k_nope, g_k_rope, b_norm, b_w_in, b_g_q_lat, b_w_uq, b_g_q_nope, b_g_q_rope, b_w_out, mem_norm, w_mem_kv, g_mem_q, g_mem_k):
    batch, seq, d = x.shape
    m = batch * seq
    mem_len = mem.shape[1]
    x2 = x.reshape(m, d)
    mem2 = mem.reshape(batch * mem_len, d)
    pos = positions.reshape(m, 1)
    consts = _rope_consts()
    row = lambda g: g.reshape(1, -1)

    def mem_kv(layer):
        w = w_mem_kv[layer].astype(BF16)
        mkv = _norm_matmul(mem2, mem_norm[layer], w, jnp.ones((2 * MEM_W,), F32), F32, 256, 512)
        return mkv.reshape(batch, mem_len, 2 * MEM_W)

    a_in_w = a_w_in.shape[-1]
    qscale = jnp.ones((a_in_w,), F32).at[:SB_W].set(HEAD_DIM ** -0.5 * LOG2_E)
    pa = _norm_matmul(x2, a_norm[0], a_w_in[0].astype(BF16), qscale, BF16, 1024, 512)
    sb = _sb_attention(pa, batch, seq)
    qm_blk = 4 * SB_W // MEM_W
    x2 = _mem_out_proj(sb, pa, qm_blk, qm_blk + 1, mem_kv(0), g_mem_q[0], g_mem_k[0],
                       x2, a_w_out[0].astype(BF16), seq, 1024, 1024)

    wd = _pad_cols(w_dkv, KV_LORA_RANK + LANES).astype(BF16)
    kx, v = _row_call(
        _kv_kernel, x2, pos, consts,
        (row(kv_norm), wd, row(g_ckv), w_ukv.astype(BF16), row(g_k_nope), _pad_cols(row(g_k_rope), LANES)),
        (N_MLA_HEADS * QK_PAD, MLA_W), 512, "mla_kv_side")

    wuq = b_w_uq[0].reshape(Q_LORA_RANK, N_MLA_HEADS, QK_NOPE_DIM + QK_ROPE_DIM)
    wuq = _pad_cols(wuq, QK_PAD).reshape(Q_LORA_RANK, N_MLA_HEADS * QK_PAD).astype(BF16)
    qx, side = _row_call(
        _q_kernel, x2, pos, consts,
        (row(b_norm[0]), b_w_in[0].astype(BF16), row(b_g_q_lat[0]), wuq, row(b_g_q_nope[0]),
         _pad_cols(row(b_g_q_rope[0]), LANES)),
        (N_MLA_HEADS * QK_PAD, MLA_W + 2 * MEM_W), 256, "mla_q_side")
    att = _mla_attention(qx, kx, v, side, batch, seq)
    qm_blk = MLA_W // MEM_W
    x2 = _mem_out_proj(att, side, qm_blk, qm_blk + 1, mem_kv(1), g_mem_q[1], g_mem_k[1],
                       x2, b_w_out[0].astype(BF16), seq, 1024, 1024)
    return x2.reshape(batch, seq, d)
```

```python
import functools
import math

import jax
import jax.numpy as jnp
import numpy as np
from jax import lax
from jax.experimental import pallas as pl
from jax.experimental.pallas import tpu as pltpu

F32 = jnp.float32
BF16 = jnp.bfloat16

HEAD_DIM = 128
N_SB_HEADS = 12
N_MEM_HEADS = 4
N_MLA_HEADS = 12
Q_LORA_RANK = 512
KV_LORA_RANK = 512
QK_NOPE_DIM = 128
QK_ROPE_DIM = 64
V_HEAD_DIM = 128
ROPE_THETA = 10000.0
EPS = 1e-6
SB_W = N_SB_HEADS * HEAD_DIM
MEM_W = N_MEM_HEADS * HEAD_DIM
MLA_W = N_MLA_HEADS * V_HEAD_DIM
LOG2_E = math.log2(math.e)
LANES = 128
SUBLANES = 8
QK_PAD = 2 * LANES
VMEM_LIMIT = 56 * 1024 * 1024

SB_TQ, SB_TK = 1024, 256
MLA_TQ, MLA_TK = 2048, 512
SB_SUB = 2 * LANES
EXP2_CLAMP = 126.0
SB_DEAD_LOG2 = 150.0
MLA_SAFE_BOUND = 60.0


def _rms_scale(x, width=None):
    ss = jnp.sum(x * x, axis=-1, keepdims=True)
    return lax.rsqrt(ss / (x.shape[-1] if width is None else width) + EPS)


def _silu(g):
    return g / (1.0 + jnp.exp(-g))


def _dot(a, b):
    return jnp.dot(a, b, preferred_element_type=F32)


def _dot_nt(a, b):
    return lax.dot_general(a, b, (((1,), (1,)), ((), ())), preferred_element_type=F32)


def _normmm_kernel(x_ref, g_ref, w_ref, cs_ref, o_ref, h_ref):
    @pl.when(pl.program_id(1) == 0)
    def _():
        x = x_ref[...]
        h_ref[...] = (x * _rms_scale(x) * g_ref[...]).astype(BF16)

    o_ref[...] = (_dot(h_ref[...], w_ref[...]) * cs_ref[...]).astype(o_ref.dtype)


def _norm_matmul(x, g, w, colscale, out_dtype, tm, tn):
    m, k = x.shape
    n = w.shape[1]
    return pl.pallas_call(
        _normmm_kernel,
        grid=(m // tm, n // tn),
        in_specs=[
            pl.BlockSpec((tm, k), lambda i, j: (i, 0)),
            pl.BlockSpec((1, k), lambda i, j: (0, 0)),
            pl.BlockSpec((k, tn), lambda i, j: (0, j)),
            pl.BlockSpec((1, tn), lambda i, j: (0, j)),
        ],
        out_specs=pl.BlockSpec((tm, tn), lambda i, j: (i, j)),
        out_shape=jax.ShapeDtypeStruct((m, n), out_dtype),
        scratch_shapes=[pltpu.VMEM((tm, k), BF16)],
        compiler_params=pltpu.CompilerParams(
            dimension_semantics=("parallel", "arbitrary"), vmem_limit_bytes=VMEM_LIMIT),
        name="norm_matmul",
    )(x, g.reshape(1, k), w, colscale.reshape(1, n))


def _mask_top(x, keep, fill):
    n = keep.shape[0]
    top = jnp.where(keep, x[:n], fill)
    return top if x.shape[0] == n else jnp.concatenate([top, x[n:]], axis=0)


def _walk_key_blocks(block, i, nd, alive=None):
    for d in reversed(range(nd)):
        block(i * nd + d, d, True)

    if alive is None:
        def body(j, carry):
            block(i * nd - 1 - j, 0, False)
            return carry

        lax.fori_loop(0, i * nd, body, 0)
    else:
        def step(state):
            block(i * nd - 1 - state[0], 0, False)
            return state[0] + 1, alive()

        lax.while_loop(lambda state: jnp.logical_and(state[0] < i * nd, state[1]), step, (0, alive()))


def _sb_kernel(q_ref, k_ref, v_ref, g_ref, u_ref, o_ref, acc_ref, carry_ref, *, tq, tk):
    i = pl.program_id(2)
    u = u_ref[...]
    row = lax.broadcasted_iota(jnp.int32, (tk, tk), 0)
    col = lax.broadcasted_iota(jnp.int32, (tk, tk), 1)
    causal = col < row
    acc_ref[...] = jnp.zeros_like(acc_ref)
    carry_ref[...] = jnp.zeros_like(carry_ref)

    def block(kb, d, diagonal):
        r0 = d * tk
        start = pl.multiple_of(kb * tk, tk)
        z = _dot_nt(q_ref[r0:, :], k_ref[pl.ds(start, tk), :])
        neg_log = jnp.maximum(z, jnp.log(1.0 + jnp.exp2(jnp.minimum(z, EXP2_CLAMP))) * LOG2_E)
        if diagonal:
            neg_log = _mask_top(neg_log, causal, 0.0)
        hi = neg_log.astype(BF16)
        seen = carry_ref[r0:, :]
        pieces = []
        for c in reversed(range(tk // SB_SUB)):
            sub = slice(c * SB_SUB, (c + 1) * SB_SUB)
            suffix = _dot(hi[:, sub], u)
            newer = jnp.concatenate([seen] * (SB_SUB // LANES), axis=1)
            pieces.append(jnp.exp2(z[:, sub] - suffix - newer))
            seen = seen + jnp.sum(neg_log[:, sub], axis=-1, keepdims=True)
        a = jnp.concatenate(pieces[::-1], axis=1)
        if diagonal:
            a = _mask_top(a, causal, 0.0)
        acc_ref[r0:, :] += _dot(a.astype(BF16), v_ref[pl.ds(start, tk), :])
        carry_ref[r0:, :] = seen

    _walk_key_blocks(block, i, tq // tk, alive=lambda: jnp.min(carry_ref[...]) < SB_DEAD_LOG2)
    o_ref[...] = (acc_ref[...] * _silu(g_ref[...].astype(F32))).astype(o_ref.dtype)


def _sb_attention(pa, batch, seq):
    tq, tk = SB_TQ, SB_TK
    nq = seq // tq
    h0 = N_SB_HEADS
    u = jnp.tril(jnp.ones((SB_SUB, SB_SUB), F32)).astype(BF16)
    return pl.pallas_call(
        functools.partial(_sb_kernel, tq=tq, tk=tk),
        grid=(batch, N_SB_HEADS, nq),
        in_specs=[
            pl.BlockSpec((tq, HEAD_DIM), lambda b, h, i: (b * nq + i, h)),
            pl.BlockSpec((seq, HEAD_DIM), lambda b, h, i: (b, h0 + h)),
            pl.BlockSpec((seq, HEAD_DIM), lambda b, h, i: (b, 2 * h0 + h)),
            pl.BlockSpec((tq, HEAD_DIM), lambda b, h, i: (b * nq + i, 3 * h0 + h)),
            pl.BlockSpec((SB_SUB, SB_SUB), lambda b, h, i: (0, 0)),
        ],
        out_specs=pl.BlockSpec((tq, HEAD_DIM), lambda b, h, i: (b * nq + i, h)),
        out_shape=jax.ShapeDtypeStruct((batch * seq, SB_W), BF16),
        scratch_shapes=[pltpu.VMEM((tq, HEAD_DIM), F32), pltpu.VMEM((tq, LANES), F32)],
        compiler_params=pltpu.CompilerParams(
            dimension_semantics=("parallel", "parallel", "arbitrary"), vmem_limit_bytes=VMEM_LIMIT),
        name="sb_attention",
    )(pa, pa, pa, pa, u)


def _mla_kernel(q_ref, k_ref, v_ref, g_ref, o_ref, acc_ref, m_ref, den_ref, knorm_ref, *, tq, tk):
    i = pl.program_id(2)
    row = lax.broadcasted_iota(jnp.int32, (tk, tk), 0)
    col = lax.broadcasted_iota(jnp.int32, (tk, tk), 1)
    causal = col <= row
    nlane = tk // LANES

    @pl.when(i == 0)
    def _():
        def chunk(c, best):
            k = k_ref[pl.ds(pl.multiple_of(c * tk, tk), tk), :].astype(F32)
            return jnp.maximum(best, jnp.sum(k * k, axis=-1, keepdims=True))

        best = lax.fori_loop(0, k_ref.shape[0] // tk, chunk, jnp.zeros((tk, 1), F32))
        knorm_ref[...] = jnp.sqrt(jnp.broadcast_to(jnp.max(best, axis=0, keepdims=True), knorm_ref.shape))

    q = q_ref[...].astype(F32)
    bound = jnp.sqrt(jnp.sum(q * q, axis=-1, keepdims=True)) * knorm_ref[0:1, :]
    acc_ref[...] = jnp.zeros_like(acc_ref)
    den_ref[...] = jnp.zeros_like(den_ref)

    def scores(kb, d, diagonal):
        start = pl.multiple_of(kb * tk, tk)
        s = _dot_nt(q_ref[d * tk:, :], k_ref[pl.ds(start, tk), :])
        return _mask_top(s, causal, -jnp.inf) if diagonal else s

    def finish(den):
        o_ref[...] = (acc_ref[...] / den * _silu(g_ref[...].astype(F32))).astype(o_ref.dtype)

    def bounded():
        m_ref[...] = bound

        def block(kb, d, diagonal):
            r0 = d * tk
            s = scores(kb, d, diagonal)
            m = m_ref[r0:, :]
            ps = [jnp.exp2(s[:, c * LANES:(c + 1) * LANES] - m) for c in range(nlane)]
            den_ref[r0:, :] += functools.reduce(lambda a, b: a + b, ps)
            p = jnp.concatenate(ps, axis=1).astype(BF16)
            acc_ref[r0:, :] += _dot(p, v_ref[pl.ds(pl.multiple_of(kb * tk, tk), tk), :])

        _walk_key_blocks(block, i, tq // tk)
        finish(jnp.sum(den_ref[...], axis=-1, keepdims=True))

    def online():
        m_ref[...] = jnp.full_like(m_ref, -jnp.inf)

        def block(kb, d, diagonal):
            r0 = d * tk
            s = scores(kb, d, diagonal)
            m_old = m_ref[r0:, :]
            m_new = jnp.maximum(m_old, jnp.max(s, axis=-1, keepdims=True))
            alpha = jnp.exp2(m_old - m_new)
            p = jnp.concatenate([jnp.exp2(s[:, c * LANES:(c + 1) * LANES] - m_new) for c in range(nlane)], axis=1)
            den_ref[r0:, :] = alpha * den_ref[r0:, :] + jnp.sum(p, axis=-1, keepdims=True)
            pv = _dot(p.astype(BF16), v_ref[pl.ds(pl.multiple_of(kb * tk, tk), tk), :])
            acc_ref[r0:, :] = alpha * acc_ref[r0:, :] + pv
            m_ref[r0:, :] = m_new

        _walk_key_blocks(block, i, tq // tk)
        finish(den_ref[...])

    lax.cond(jnp.max(bound) <= MLA_SAFE_BOUND, bounded, online)


def _mla_attention(qx, kx, v, gq, batch, seq):
    tq, tk = MLA_TQ, MLA_TK
    nq = seq // tq
    return pl.pallas_call(
        functools.partial(_mla_kernel, tq=tq, tk=tk),
        grid=(batch, N_MLA_HEADS, nq),
        in_specs=[
            pl.BlockSpec((tq, QK_PAD), lambda b, h, i: (b * nq + i, h)),
            pl.BlockSpec((seq, QK_PAD), lambda b, h, i: (b, h)),
            pl.BlockSpec((seq, V_HEAD_DIM), lambda b, h, i: (b, h)),
            pl.BlockSpec((tq, V_HEAD_DIM), lambda b, h, i: (b * nq + i, h)),
        ],
        out_specs=pl.BlockSpec((tq, V_HEAD_DIM), lambda b, h, i: (b * nq + i, h)),
        out_shape=jax.ShapeDtypeStruct((batch * seq, MLA_W), BF16),
        scratch_shapes=[pltpu.VMEM((tq, V_HEAD_DIM), F32), pltpu.VMEM((tq, LANES), F32),
                        pltpu.VMEM((tq, LANES), F32), pltpu.VMEM((SUBLANES, LANES), F32)],
        compiler_params=pltpu.CompilerParams(
            dimension_semantics=("parallel", "parallel", "arbitrary"), vmem_limit_bytes=VMEM_LIMIT),
        name="mla_attention",
    )(qx, kx, v, gq)


def _out_kernel(main_ref, qm_ref, gm_ref, mkv_ref, gq_ref, gk_ref, x_ref, w_ref, o_ref, mo_ref, *, main_w):
    @pl.when(pl.program_id(1) == 0)
    def _():
        for h in range(N_MEM_HEADS):
            lo, hi = h * HEAD_DIM, (h + 1) * HEAD_DIM
            mk = mkv_ref[0, :, lo:hi]
            mk = (mk * _rms_scale(mk) * gk_ref[...]).astype(BF16)
            mv = mkv_ref[0, :, MEM_W + lo:MEM_W + hi].astype(BF16)
            q = qm_ref[:, lo:hi].astype(F32)
            q = (q * _rms_scale(q) * gq_ref[...]).astype(BF16)
            s = _dot_nt(q, mk) * HEAD_DIM ** -0.5
            p = jnp.exp(s - jnp.max(s, axis=-1, keepdims=True))
            mo = _dot(p.astype(BF16), mv) / jnp.sum(p, axis=-1, keepdims=True)
            mo_ref[:, lo:hi] = (mo * _silu(gm_ref[:, lo:hi].astype(F32))).astype(BF16)

    acc = _dot(main_ref[...], w_ref[:main_w, :]) + _dot(mo_ref[...], w_ref[main_w:, :])
    o_ref[...] = x_ref[...] + acc


def _mem_out_proj(main, side, qm_blk, gm_blk, mkv, g_q, g_k, x, w, seq, tm, tn):
    m, main_w = main.shape
    d = w.shape[1]
    mem_len = mkv.shape[1]
    per_b = seq // tm
    return pl.pallas_call(
        functools.partial(_out_kernel, main_w=main_w),
        grid=(m // tm, d // tn),
        in_specs=[
            pl.BlockSpec((tm, main_w), lambda i, j: (i, 0)),
            pl.BlockSpec((tm, MEM_W), lambda i, j: (i, qm_blk)),
            pl.BlockSpec((tm, MEM_W), lambda i, j: (i, gm_blk)),
            pl.BlockSpec((1, mem_len, 2 * MEM_W), lambda i, j: (i // per_b, 0, 0)),
            pl.BlockSpec((1, HEAD_DIM), lambda i, j: (0, 0)),
            pl.BlockSpec((1, HEAD_DIM), lambda i, j: (0, 0)),
            pl.BlockSpec((tm, tn), lambda i, j: (i, j)),
            pl.BlockSpec((main_w + MEM_W, tn), lambda i, j: (0, j)),
        ],
        out_specs=pl.BlockSpec((tm, tn), lambda i, j: (i, j)),
        out_shape=jax.ShapeDtypeStruct((m, d), F32),
        scratch_shapes=[pltpu.VMEM((tm, MEM_W), BF16)],
        compiler_params=pltpu.CompilerParams(
            dimension_semantics=("parallel", "arbitrary"), vmem_limit_bytes=VMEM_LIMIT),
        name="mem_out_proj",
    )(main, side, side, mkv, g_q.reshape(1, HEAD_DIM), g_k.reshape(1, HEAD_DIM), x, w)


def _rope_tables(pos_ref, c_ref):
    ang = pos_ref[...].astype(F32) * c_ref[0:1, :]
    cosv, sinv = jnp.cos(ang), jnp.sin(ang)
    return cosv * c_ref[1:2, :], sinv * c_ref[2:3, :], sinv * c_ref[3:4, :]


def _rope(x, tables):
    c, s1, s2 = tables
    return x * c + pltpu.roll(x, LANES - QK_ROPE_DIM // 2, 1) * s1 + pltpu.roll(x, QK_ROPE_DIM // 2, 1) * s2


def _kv_kernel(x_ref, pos_ref, c_ref, gx_ref, wd_ref, gc_ref, wu_ref, gkn_ref, gkr_ref, kx_ref, v_ref):
    x = x_ref[...]
    h = (x * _rms_scale(x) * gx_ref[...]).astype(BF16)
    c = _dot(h, wd_ref[...])
    ckv = c[:, :KV_LORA_RANK]
    cn = (ckv * _rms_scale(ckv) * gc_ref[...]).astype(BF16)
    kv = _dot(cn, wu_ref[...])
    kr = c[:, KV_LORA_RANK:]
    kr = kr * _rms_scale(kr, QK_ROPE_DIM) * gkr_ref[...]
    k_rope = _rope(kr, _rope_tables(pos_ref, c_ref)).astype(BF16)
    for hd in range(N_MLA_HEADS):
        base = hd * QK_PAD
        kn = kv[:, base:base + QK_NOPE_DIM]
        kx_ref[:, base:base + QK_NOPE_DIM] = (kn * _rms_scale(kn) * gkn_ref[...]).astype(BF16)
        kx_ref[:, base + QK_NOPE_DIM:base + QK_PAD] = k_rope
        v_ref[:, hd * V_HEAD_DIM:(hd + 1) * V_HEAD_DIM] = kv[:, base + QK_NOPE_DIM:base + QK_PAD].astype(BF16)


def _q_kernel(x_ref, pos_ref, c_ref, gx_ref, win_ref, gl_ref, wuq_ref, gqn_ref, gqr_ref, qx_ref, side_ref):
    x = x_ref[...]
    h = (x * _rms_scale(x) * gx_ref[...]).astype(BF16)
    p = _dot(h, win_ref[...])
    side_ref[...] = p[:, Q_LORA_RANK:].astype(BF16)
    ql = p[:, :Q_LORA_RANK]
    qn = (ql * _rms_scale(ql) * gl_ref[...]).astype(BF16)
    q = _dot(qn, wuq_ref[...])
    tables = _rope_tables(pos_ref, c_ref)
    scale = (QK_NOPE_DIM + QK_ROPE_DIM) ** -0.5 * LOG2_E
    for hd in range(N_MLA_HEADS):
        base = hd * QK_PAD
        qnope = q[:, base:base + QK_NOPE_DIM]
        qx_ref[:, base:base + QK_NOPE_DIM] = (qnope * _rms_scale(qnope) * gqn_ref[...] * scale).astype(BF16)
        qr = q[:, base + QK_NOPE_DIM:base + QK_PAD]
        qr = qr * _rms_scale(qr, QK_ROPE_DIM) * gqr_ref[...]
        qx_ref[:, base + QK_NOPE_DIM:base + QK_PAD] = (_rope(qr, tables) * scale).astype(BF16)


def _row_call(body, x, pos, consts, small, outs, tm, name):
    m, d = x.shape
    resident = [pl.BlockSpec(a.shape, lambda i: (0, 0), pipeline_mode=pl.Buffered(1)) for a in (consts, *small)]
    return pl.pallas_call(
        body,
        grid=(m // tm,),
        in_specs=[pl.BlockSpec((tm, d), lambda i: (i, 0)), pl.BlockSpec((tm, 1), lambda i: (i, 0))] + resident,
        out_specs=[pl.BlockSpec((tm, w), lambda i: (i, 0)) for w in outs],
        out_shape=[jax.ShapeDtypeStruct((m, w), BF16) for w in outs],
        compiler_params=pltpu.CompilerParams(
            dimension_semantics=("parallel",), vmem_limit_bytes=VMEM_LIMIT),
        name=name,
    )(x, pos, consts, *small)


def _rope_consts():
    half = QK_ROPE_DIM // 2
    inv_freq = jnp.power(ROPE_THETA, -jnp.arange(0, QK_ROPE_DIM, 2, dtype=F32) / QK_ROPE_DIM)
    lane = np.arange(LANES)
    rows = jnp.zeros((8, LANES), F32)
    rows = rows.at[0, :QK_ROPE_DIM].set(jnp.concatenate([inv_freq, inv_freq]))
    rows = rows.at[1].set(jnp.asarray(lane < QK_ROPE_DIM, F32))
    rows = rows.at[2].set(jnp.asarray(-(lane < half).astype(np.float32)))
    rows = rows.at[3].set(jnp.asarray(((lane >= half) & (lane < QK_ROPE_DIM)).astype(np.float32)))
    return rows


def _pad_cols(a, width):
    return jnp.pad(a, [(0, 0)] * (a.ndim - 1) + [(0, width - a.shape[-1])])


def kernel(x, mem, positions, a_norm, a_w_in, a_w_out, kv_norm, w_dkv, g_ckv, w_ukv, g_k_nope, g_k_rope, b_norm, b_w_in, b_g_q_lat, b_w_uq, b_g_q_nope, b_g_q_rope, b_w_out, mem_norm, w_mem_kv, g_mem_q, g_mem_k):
    batch, seq, d = x.shape
    m = batch * seq
    mem_len = mem.shape[1]
    x2 = x.reshape(m, d)
    mem2 = mem.reshape(batch * mem_len, d)
    pos = positions.reshape(m, 1)
    consts = _rope_consts()
    row = lambda g: g.reshape(1, -1)

    def mem_kv(layer):
        w = w_mem_kv[layer].astype(BF16)
        mkv = _norm_matmul(mem2, mem_norm[layer], w, jnp.ones((2 * MEM_W,), F32), F32, 256, 512)
        return mkv.reshape(batch, mem_len, 2 * MEM_W)

    a_in_w = a_w_in.shape[-1]
    qscale = jnp.ones((a_in_w,), F32).at[:SB_W].set(HEAD_DIM ** -0.5 * LOG2_E)
    pa = _norm_matmul(x2, a_norm[0], a_w_in[0].astype(BF16), qscale, BF16, 1024, 512)
    sb = _sb_attention(pa, batch, seq)
    qm_blk = 4 * SB_W // MEM_W
    x2 = _mem_out_proj(sb, pa, qm_blk, qm_blk + 1, mem_kv(0), g_mem_q[0], g_mem_k[0],
                       x2, a_w_out[0].astype(BF16), seq, 1024, 1024)

    wd = _pad_cols(w_dkv, KV_LORA_RANK + LANES).astype(BF16)
    kx, v = _row_call(
        _kv_kernel, x2, pos, consts,
        (row(kv_norm), wd, row(g_ckv), w_ukv.astype(BF16), row(g_k_nope), _pad_cols(row(g_k_rope), LANES)),
        (N_MLA_HEADS * QK_PAD, MLA_W), 512, "mla_kv_side")

    wuq = b_w_uq[0].reshape(Q_LORA_RANK, N_MLA_HEADS, QK_NOPE_DIM + QK_ROPE_DIM)
    wuq = _pad_cols(wuq, QK_PAD).reshape(Q_LORA_RANK, N_MLA_HEADS * QK_PAD).astype(BF16)
    qx, side = _row_call(
        _q_kernel, x2, pos, consts,
        (row(b_norm[0]), b_w_in[0].astype(BF16), row(b_g_q_lat[0]), wuq, row(b_g_q_nope[0]),
         _pad_cols(row(b_g_q_rope[0]), LANES)),
        (N_MLA_HEADS * QK_PAD, MLA_W + 2 * MEM_W), 256, "mla_q_side")
    att = _mla_attention(qx, kx, v, side, batch, seq)
    qm_blk = MLA_W // MEM_W
    x2 = _mem_out_proj(att, side, qm_blk, qm_blk + 1, mem_kv(1), g_mem_q[1], g_mem_k[1],
                       x2, b_w_out[0].astype(BF16), seq, 1024, 1024)
    return x2.reshape(batch, seq, d)
```

```python
import functools
import math

import jax
import jax.numpy as jnp
import numpy as np
from jax import lax
from jax.experimental import pallas as pl
from jax.experimental.pallas import tpu as pltpu

F32 = jnp.float32
BF16 = jnp.bfloat16

HEAD_DIM = 128
N_SB_HEADS = 12
N_MEM_HEADS = 4
N_MLA_HEADS = 12
Q_LORA_RANK = 512
KV_LORA_RANK = 512
QK_NOPE_DIM = 128
QK_ROPE_DIM = 64
V_HEAD_DIM = 128
ROPE_THETA = 10000.0
EPS = 1e-6
SB_W = N_SB_HEADS * HEAD_DIM
MEM_W = N_MEM_HEADS * HEAD_DIM
MLA_W = N_MLA_HEADS * V_HEAD_DIM
LOG2_E = math.log2(math.e)
LANES = 128
SUBLANES = 8
QK_PAD = 2 * LANES
VMEM_LIMIT = 56 * 1024 * 1024

INPROJ_TM, INPROJ_NSPLIT = 512, 2
MEMKV_TILES = (256, 512)
OUT_TM = 512
KV_TM = 1024
Q_TM = 256
SB_TQ, SB_TK = 1024, 256
MLA_TQ, MLA_TK = 2048, 512
SB_SUB = 2 * LANES
EXP2_CLAMP = 126.0
SB_DEAD_LOG2 = 150.0
MLA_SAFE_BOUND = 60.0


def _rms_scale(x, width=None):
    ss = jnp.sum(x * x, axis=-1, keepdims=True)
    return lax.rsqrt(ss / (x.shape[-1] if width is None else width) + EPS)


def _silu(g):
    return g / (1.0 + jnp.exp(-g))


def _dot(a, b):
    return jnp.dot(a, b, preferred_element_type=F32)


def _dot_nt(a, b):
    return lax.dot_general(a, b, (((1,), (1,)), ((), ())), preferred_element_type=F32)


def _normmm_kernel(x_ref, g_ref, w_ref, cs_ref, o_ref, h_ref):
    @pl.when(pl.program_id(1) == 0)
    def _():
        x = x_ref[...]
        h_ref[...] = (x * _rms_scale(x) * g_ref[...]).astype(BF16)

    o_ref[...] = (_dot(h_ref[...], w_ref[...]) * cs_ref[...]).astype(o_ref.dtype)


def _norm_matmul(x, g, w, colscale, out_dtype, tm, tn):
    m, k = x.shape
    n = w.shape[1]
    return pl.pallas_call(
        _normmm_kernel,
        grid=(m // tm, n // tn),
        in_specs=[
            pl.BlockSpec((tm, k), lambda i, j: (i, 0)),
            pl.BlockSpec((1, k), lambda i, j: (0, 0)),
            pl.BlockSpec((k, tn), lambda i, j: (0, j)),
            pl.BlockSpec((1, tn), lambda i, j: (0, j)),
        ],
        out_specs=pl.BlockSpec((tm, tn), lambda i, j: (i, j)),
        out_shape=jax.ShapeDtypeStruct((m, n), out_dtype),
        scratch_shapes=[pltpu.VMEM((tm, k), BF16)],
        compiler_params=pltpu.CompilerParams(
            dimension_semantics=("parallel", "arbitrary"), vmem_limit_bytes=VMEM_LIMIT),
        name="norm_matmul",
    )(x, g.reshape(1, k), w, colscale.reshape(1, n))


def _mask_top(x, keep, fill):
    n = keep.shape[0]
    top = jnp.where(keep, x[:n], fill)
    return top if x.shape[0] == n else jnp.concatenate([top, x[n:]], axis=0)


def _walk_key_blocks(block, i, nd, alive=None):
    for d in reversed(range(nd)):
        block(i * nd + d, d, True)

    if alive is None:
        def body(j, carry):
            block(i * nd - 1 - j, 0, False)
            return carry

        lax.fori_loop(0, i * nd, body, 0)
    else:
        def step(state):
            block(i * nd - 1 - state[0], 0, False)
            return state[0] + 1, alive()

        lax.while_loop(lambda state: jnp.logical_and(state[0] < i * nd, state[1]), step, (0, alive()))


def _sb_kernel(q_ref, k_ref, v_ref, g_ref, u_ref, o_ref, acc_ref, carry_ref, *, tq, tk):
    i = pl.program_id(2)
    u = u_ref[...]
    row = lax.broadcasted_iota(jnp.int32, (tk, tk), 0)
    col = lax.broadcasted_iota(jnp.int32, (tk, tk), 1)
    causal = col < row
    acc_ref[...] = jnp.zeros_like(acc_ref)
    carry_ref[...] = jnp.zeros_like(carry_ref)

    def block(kb, d, diagonal):
        r0 = d * tk
        start = pl.multiple_of(kb * tk, tk)
        z = _dot_nt(q_ref[r0:, :], k_ref[pl.ds(start, tk), :])
        neg_log = jnp.maximum(z, jnp.log(1.0 + jnp.exp2(jnp.minimum(z, EXP2_CLAMP))) * LOG2_E)
        if diagonal:
            neg_log = _mask_top(neg_log, causal, 0.0)
        hi = neg_log.astype(BF16)
        seen = carry_ref[r0:, :]
        pieces = []
        for c in reversed(range(tk // SB_SUB)):
            sub = slice(c * SB_SUB, (c + 1) * SB_SUB)
            suffix = _dot(hi[:, sub], u)
            newer = jnp.concatenate([seen] * (SB_SUB // LANES), axis=1)
            pieces.append(jnp.exp2(z[:, sub] - suffix - newer))
            seen = seen + jnp.sum(neg_log[:, sub], axis=-1, keepdims=True)
        a = jnp.concatenate(pieces[::-1], axis=1)
        if diagonal:
            a = _mask_top(a, causal, 0.0)
        acc_ref[r0:, :] += _dot(a.astype(BF16), v_ref[pl.ds(start, tk), :])
        carry_ref[r0:, :] = seen

    _walk_key_blocks(block, i, tq // tk, alive=lambda: jnp.min(carry_ref[...]) < SB_DEAD_LOG2)
    o_ref[...] = (acc_ref[...] * _silu(g_ref[...].astype(F32))).astype(o_ref.dtype)


def _sb_attention(pa, batch, seq):
    tq, tk = SB_TQ, SB_TK
    nq = seq // tq
    h0 = N_SB_HEADS
    u = jnp.tril(jnp.ones((SB_SUB, SB_SUB), F32)).astype(BF16)
    return pl.pallas_call(
        functools.partial(_sb_kernel, tq=tq, tk=tk),
        grid=(batch, N_SB_HEADS, nq),
        in_specs=[
            pl.BlockSpec((tq, HEAD_DIM), lambda b, h, i: (b * nq + i, h)),
            pl.BlockSpec((seq, HEAD_DIM), lambda b, h, i: (b, h0 + h)),
            pl.BlockSpec((seq, HEAD_DIM), lambda b, h, i: (b, 2 * h0 + h)),
            pl.BlockSpec((tq, HEAD_DIM), lambda b, h, i: (b * nq + i, 3 * h0 + h)),
            pl.BlockSpec((SB_SUB, SB_SUB), lambda b, h, i: (0, 0)),
        ],
        out_specs=pl.BlockSpec((tq, HEAD_DIM), lambda b, h, i: (b * nq + i, h)),
        out_shape=jax.ShapeDtypeStruct((batch * seq, SB_W), BF16),
        scratch_shapes=[pltpu.VMEM((tq, HEAD_DIM), F32), pltpu.VMEM((tq, LANES), F32)],
        compiler_params=pltpu.CompilerParams(
            dimension_semantics=("parallel", "parallel", "arbitrary"), vmem_limit_bytes=VMEM_LIMIT),
        name="sb_attention",
    )(pa, pa, pa, pa, u)


def _mla_kernel(q_ref, k_ref, v_ref, g_ref, o_ref, acc_ref, m_ref, den_ref, knorm_ref, *, tq, tk):
    i = pl.program_id(2)
    row = lax.broadcasted_iota(jnp.int32, (tk, tk), 0)
    col = lax.broadcasted_iota(jnp.int32, (tk, tk), 1)
    causal = col <= row
    nlane = tk // LANES

    @pl.when(i == 0)
    def _():
        def chunk(c, best):
            k = k_ref[pl.ds(pl.multiple_of(c * tk, tk), tk), :].astype(F32)
            return jnp.maximum(best, jnp.sum(k * k, axis=-1, keepdims=True))

        best = lax.fori_loop(0, k_ref.shape[0] // tk, chunk, jnp.zeros((tk, 1), F32))
        knorm_ref[...] = jnp.sqrt(jnp.broadcast_to(jnp.max(best, axis=0, keepdims=True), knorm_ref.shape))

    q = q_ref[...].astype(F32)
    bound = jnp.sqrt(jnp.sum(q * q, axis=-1, keepdims=True)) * knorm_ref[0:1, :]
    acc_ref[...] = jnp.zeros_like(acc_ref)
    den_ref[...] = jnp.zeros_like(den_ref)

    def scores(kb, d, diagonal):
        start = pl.multiple_of(kb * tk, tk)
        s = _dot_nt(q_ref[d * tk:, :], k_ref[pl.ds(start, tk), :])
        return _mask_top(s, causal, -jnp.inf) if diagonal else s

    def finish(den):
        o_ref[...] = (acc_ref[...] / den * _silu(g_ref[...].astype(F32))).astype(o_ref.dtype)

    def bounded():
        m_ref[...] = bound

        def block(kb, d, diagonal):
            r0 = d * tk
            s = scores(kb, d, diagonal)
            m = m_ref[r0:, :]
            ps = [jnp.exp2(s[:, c * LANES:(c + 1) * LANES] - m) for c in range(nlane)]
            den_ref[r0:, :] += functools.reduce(lambda a, b: a + b, ps)
            p = jnp.concatenate(ps, axis=1).astype(BF16)
            acc_ref[r0:, :] += _dot(p, v_ref[pl.ds(pl.multiple_of(kb * tk, tk), tk), :])

        _walk_key_blocks(block, i, tq // tk)
        finish(jnp.sum(den_ref[...], axis=-1, keepdims=True))

    def online():
        m_ref[...] = jnp.full_like(m_ref, -jnp.inf)

        def block(kb, d, diagonal):
            r0 = d * tk
            s = scores(kb, d, diagonal)
            m_old = m_ref[r0:, :]
            m_new = jnp.maximum(m_old, jnp.max(s, axis=-1, keepdims=True))
            alpha = jnp.exp2(m_old - m_new)
            p = jnp.concatenate([jnp.exp2(s[:, c * LANES:(c + 1) * LANES] - m_new) for c in range(nlane)], axis=1)
            den_ref[r0:, :] = alpha * den_ref[r0:, :] + jnp.sum(p, axis=-1, keepdims=True)
            pv = _dot(p.astype(BF16), v_ref[pl.ds(pl.multiple_of(kb * tk, tk), tk), :])
            acc_ref[r0:, :] = alpha * acc_ref[r0:, :] + pv
            m_ref[r0:, :] = m_new

        _walk_key_blocks(block, i, tq // tk)
        finish(den_ref[...])

    lax.cond(jnp.max(bound) <= MLA_SAFE_BOUND, bounded, online)


def _mla_attention(qx, kx, v, gq, batch, seq):
    tq, tk = MLA_TQ, MLA_TK
    nq = seq // tq
    return pl.pallas_call(
        functools.partial(_mla_kernel, tq=tq, tk=tk),
        grid=(batch, N_MLA_HEADS, nq),
        in_specs=[
            pl.BlockSpec((tq, QK_PAD), lambda b, h, i: (b * nq + i, h)),
            pl.BlockSpec((seq, QK_PAD), lambda b, h, i: (b, h)),
            pl.BlockSpec((seq, V_HEAD_DIM), lambda b, h, i: (b, h)),
            pl.BlockSpec((tq, V_HEAD_DIM), lambda b, h, i: (b * nq + i, h)),
        ],
        out_specs=pl.BlockSpec((tq, V_HEAD_DIM), lambda b, h, i: (b * nq + i, h)),
        out_shape=jax.ShapeDtypeStruct((batch * seq, MLA_W), BF16),
        scratch_shapes=[pltpu.VMEM((tq, V_HEAD_DIM), F32), pltpu.VMEM((tq, LANES), F32),
                        pltpu.VMEM((tq, LANES), F32), pltpu.VMEM((SUBLANES, LANES), F32)],
        compiler_params=pltpu.CompilerParams(
            dimension_semantics=("parallel", "parallel", "arbitrary"), vmem_limit_bytes=VMEM_LIMIT),
        name="mla_attention",
    )(qx, kx, v, gq)


def _out_kernel(main_ref, qm_ref, gm_ref, mkv_ref, gq_ref, gk_ref, x_ref, w_ref, o_ref, *, main_w):
    heads = []
    for h in range(N_MEM_HEADS):
        lo, hi = h * HEAD_DIM, (h + 1) * HEAD_DIM
        mk = mkv_ref[0, :, lo:hi]
        mk = (mk * _rms_scale(mk) * gk_ref[...]).astype(BF16)
        mv = mkv_ref[0, :, MEM_W + lo:MEM_W + hi].astype(BF16)
        q = qm_ref[:, lo:hi].astype(F32)
        q = (q * _rms_scale(q) * gq_ref[...]).astype(BF16)
        s = _dot_nt(q, mk) * HEAD_DIM ** -0.5
        p = jnp.exp(s - jnp.max(s, axis=-1, keepdims=True))
        mo = _dot(p.astype(BF16), mv) / jnp.sum(p, axis=-1, keepdims=True)
        heads.append((mo * _silu(gm_ref[:, lo:hi].astype(F32))).astype(BF16))
    acc = _dot(main_ref[...], w_ref[:main_w, :]) + _dot(jnp.concatenate(heads, axis=1), w_ref[main_w:, :])
    o_ref[...] = x_ref[...] + acc


def _mem_out_proj(main, side, qm_blk, gm_blk, mkv, g_q, g_k, x, w, seq):
    m, main_w = main.shape
    d = w.shape[1]
    mem_len = mkv.shape[1]
    tm = OUT_TM
    per_b = seq // tm
    return pl.pallas_call(
        functools.partial(_out_kernel, main_w=main_w),
        grid=(m // tm,),
        in_specs=[
            pl.BlockSpec((tm, main_w), lambda i: (i, 0)),
            pl.BlockSpec((tm, MEM_W), lambda i: (i, qm_blk)),
            pl.BlockSpec((tm, MEM_W), lambda i: (i, gm_blk)),
            pl.BlockSpec((1, mem_len, 2 * MEM_W), lambda i: (i // per_b, 0, 0)),
            pl.BlockSpec((1, HEAD_DIM), lambda i: (0, 0)),
            pl.BlockSpec((1, HEAD_DIM), lambda i: (0, 0)),
            pl.BlockSpec((tm, d), lambda i: (i, 0)),
            pl.BlockSpec((main_w + MEM_W, d), lambda i: (0, 0), pipeline_mode=pl.Buffered(1)),
        ],
        out_specs=pl.BlockSpec((tm, d), lambda i: (i, 0)),
        out_shape=jax.ShapeDtypeStruct((m, d), F32),
        compiler_params=pltpu.CompilerParams(
            dimension_semantics=("parallel",), vmem_limit_bytes=VMEM_LIMIT),
        name="mem_out_proj",
    )(main, side, side, mkv, g_q.reshape(1, HEAD_DIM), g_k.reshape(1, HEAD_DIM), x, w)


def _rope_tables(pos_ref, c_ref):
    ang = pos_ref[...].astype(F32) * c_ref[0:1, :]
    cosv, sinv = jnp.cos(ang), jnp.sin(ang)
    return cosv * c_ref[1:2, :], sinv * c_ref[2:3, :], sinv * c_ref[3:4, :]


def _rope(x, tables):
    c, s1, s2 = tables
    return x * c + pltpu.roll(x, LANES - QK_ROPE_DIM // 2, 1) * s1 + pltpu.roll(x, QK_ROPE_DIM // 2, 1) * s2


def _kv_kernel(x_ref, pos_ref, c_ref, gx_ref, wd_ref, gc_ref, wu_ref, gkn_ref, gkr_ref, kx_ref, v_ref):
    x = x_ref[...]
    h = (x * _rms_scale(x) * gx_ref[...]).astype(BF16)
    c = _dot(h, wd_ref[...])
    ckv = c[:, :KV_LORA_RANK]
    cn = (ckv * _rms_scale(ckv) * gc_ref[...]).astype(BF16)
    kv = _dot(cn, wu_ref[...])
    kr = c[:, KV_LORA_RANK:]
    kr = kr * _rms_scale(kr, QK_ROPE_DIM) * gkr_ref[...]
    k_rope = _rope(kr, _rope_tables(pos_ref, c_ref)).astype(BF16)
    for hd in range(N_MLA_HEADS):
        base = hd * QK_PAD
        kn = kv[:, base:base + QK_NOPE_DIM]
        kx_ref[:, base:base + QK_NOPE_DIM] = (kn * _rms_scale(kn) * gkn_ref[...]).astype(BF16)
        kx_ref[:, base + QK_NOPE_DIM:base + QK_PAD] = k_rope
        v_ref[:, hd * V_HEAD_DIM:(hd + 1) * V_HEAD_DIM] = kv[:, base + QK_NOPE_DIM:base + QK_PAD].astype(BF16)


def _q_kernel(x_ref, pos_ref, c_ref, gx_ref, win_ref, gl_ref, wuq_ref, gqn_ref, gqr_ref, qx_ref, side_ref):
    x = x_ref[...]
    h = (x * _rms_scale(x) * gx_ref[...]).astype(BF16)
    p = _dot(h, win_ref[...])
    side_ref[...] = p[:, Q_LORA_RANK:].astype(BF16)
    ql = p[:, :Q_LORA_RANK]
    qn = (ql * _rms_scale(ql) * gl_ref[...]).astype(BF16)
    q = _dot(qn, wuq_ref[...])
    tables = _rope_tables(pos_ref, c_ref)
    scale = (QK_NOPE_DIM + QK_ROPE_DIM) ** -0.5 * LOG2_E
    for hd in range(N_MLA_HEADS):
        base = hd * QK_PAD
        qnope = q[:, base:base + QK_NOPE_DIM]
        qx_ref[:, base:base + QK_NOPE_DIM] = (qnope * _rms_scale(qnope) * gqn_ref[...] * scale).astype(BF16)
        qr = q[:, base + QK_NOPE_DIM:base + QK_PAD]
        qr = qr * _rms_scale(qr, QK_ROPE_DIM) * gqr_ref[...]
        qx_ref[:, base + QK_NOPE_DIM:base + QK_PAD] = (_rope(qr, tables) * scale).astype(BF16)


def _row_call(body, x, pos, consts, small, outs, tm, name):
    m, d = x.shape
    resident = [pl.BlockSpec(a.shape, lambda i: (0, 0), pipeline_mode=pl.Buffered(1)) for a in (consts, *small)]
    return pl.pallas_call(
        body,
        grid=(m // tm,),
        in_specs=[pl.BlockSpec((tm, d), lambda i: (i, 0)), pl.BlockSpec((tm, 1), lambda i: (i, 0))] + resident,
        out_specs=[pl.BlockSpec((tm, w), lambda i: (i, 0)) for w in outs],
        out_shape=[jax.ShapeDtypeStruct((m, w), BF16) for w in outs],
        compiler_params=pltpu.CompilerParams(
            dimension_semantics=("parallel",), vmem_limit_bytes=VMEM_LIMIT),
        name=name,
    )(x, pos, consts, *small)


def _rope_consts():
    half = QK_ROPE_DIM // 2
    inv_freq = jnp.power(ROPE_THETA, -jnp.arange(0, QK_ROPE_DIM, 2, dtype=F32) / QK_ROPE_DIM)
    lane = np.arange(LANES)
    rows = jnp.zeros((8, LANES), F32)
    rows = rows.at[0, :QK_ROPE_DIM].set(jnp.concatenate([inv_freq, inv_freq]))
    rows = rows.at[1].set(jnp.asarray(lane < QK_ROPE_DIM, F32))
    rows = rows.at[2].set(jnp.asarray(-(lane < half).astype(np.float32)))
    rows = rows.at[3].set(jnp.asarray(((lane >= half) & (lane < QK_ROPE_DIM)).astype(np.float32)))
    return rows


def _pad_cols(a, width):
    return jnp.pad(a, [(0, 0)] * (a.ndim - 1) + [(0, width - a.shape[-1])])


def kernel(x, mem, positions, a_norm, a_w_in, a_w_out, kv_norm, w_dkv, g_ckv, w_ukv, g_k_nope, g_k_rope, b_norm, b_w_in, b_g_q_lat, b_w_uq, b_g_q_nope, b_g_q_rope, b_w_out, mem_norm, w_mem_kv, g_mem_q, g_mem_k):
    batch, seq, d = x.shape
    m = batch * seq
    mem_len = mem.shape[1]
    x2 = x.reshape(m, d)
    mem2 = mem.reshape(batch * mem_len, d)
    pos = positions.reshape(m, 1)
    consts = _rope_consts()
    row = lambda g: g.reshape(1, -1)

    def mem_kv(layer):
        w = w_mem_kv[layer].astype(BF16)
        mkv = _norm_matmul(mem2, mem_norm[layer], w, jnp.ones((2 * MEM_W,), F32), F32, *MEMKV_TILES)
        return mkv.reshape(batch, mem_len, 2 * MEM_W)

    a_in_w = a_w_in.shape[-1]
    qscale = jnp.ones((a_in_w,), F32).at[:SB_W].set(HEAD_DIM ** -0.5 * LOG2_E)
    pa = _norm_matmul(x2, a_norm[0], a_w_in[0].astype(BF16), qscale, BF16, INPROJ_TM, a_in_w // INPROJ_NSPLIT)
    sb = _sb_attention(pa, batch, seq)
    qm_blk = 4 * SB_W // MEM_W
    x2 = _mem_out_proj(sb, pa, qm_blk, qm_blk + 1, mem_kv(0), g_mem_q[0], g_mem_k[0],
                       x2, a_w_out[0].astype(BF16), seq)

    wd = _pad_cols(w_dkv, KV_LORA_RANK + LANES).astype(BF16)
    kx, v = _row_call(
        _kv_kernel, x2, pos, consts,
        (row(kv_norm), wd, row(g_ckv), w_ukv.astype(BF16), row(g_k_nope), _pad_cols(row(g_k_rope), LANES)),
        (N_MLA_HEADS * QK_PAD, MLA_W), KV_TM, "mla_kv_side")

    wuq = b_w_uq[0].reshape(Q_LORA_RANK, N_MLA_HEADS, QK_NOPE_DIM + QK_ROPE_DIM)
    wuq = _pad_cols(wuq, QK_PAD).reshape(Q_LORA_RANK, N_MLA_HEADS * QK_PAD).astype(BF16)
    qx, side = _row_call(
        _q_kernel, x2, pos, consts,
        (row(b_norm[0]), b_w_in[0].astype(BF16), row(b_g_q_lat[0]), wuq, row(b_g_q_nope[0]),
         _pad_cols(row(b_g_q_rope[0]), LANES)),
        (N_MLA_HEADS * QK_PAD, MLA_W + 2 * MEM_W), Q_TM, "mla_q_side")
    att = _mla_attention(qx, kx, v, side, batch, seq)
    qm_blk = MLA_W // MEM_W
    x2 = _mem_out_proj(att, side, qm_blk, qm_blk + 1, mem_kv(1), g_mem_q[1], g_mem_k[1],
                       x2, b_w_out[0].astype(BF16), seq)
    return x2.reshape(batch, seq, d)
```

```python
import functools
import math

import jax
import jax.numpy as jnp
import numpy as np
from jax import lax
from jax.experimental import pallas as pl
from jax.experimental.pallas import tpu as pltpu

F32 = jnp.float32
BF16 = jnp.bfloat16

HEAD_DIM = 128
N_SB_HEADS = 12
N_MEM_HEADS = 4
N_MLA_HEADS = 12
Q_LORA_RANK = 512
KV_LORA_RANK = 512
QK_NOPE_DIM = 128
QK_ROPE_DIM = 64
V_HEAD_DIM = 128
ROPE_THETA = 10000.0
EPS = 1e-6
SB_W = N_SB_HEADS * HEAD_DIM
MEM_W = N_MEM_HEADS * HEAD_DIM
MLA_W = N_MLA_HEADS * V_HEAD_DIM
LOG2_E = math.log2(math.e)
LANES = 128
SUBLANES = 8
QK_PAD = 2 * LANES
VMEM_LIMIT = 56 * 1024 * 1024

INPROJ_TM, INPROJ_NSPLIT = 512, 2
MEMKV_TILES = (256, 512)
OUT_TM = 512
KV_TM = 1024
Q_TM = 256
SB_TQ = 2048
MLA_TQ, MLA_TK = 2048, 512
SB_SUB = 2 * LANES
EXP2_CLAMP = 126.0
SB_DEAD_LOG2 = 150.0
SB_NO_KEYS_LEFT = 1e30
MLA_SAFE_BOUND = 60.0


def _rms_scale(x, width=None):
    ss = jnp.sum(x * x, axis=-1, keepdims=True)
    return lax.rsqrt(ss / (x.shape[-1] if width is None else width) + EPS)


def _silu(g):
    return g / (1.0 + jnp.exp(-g))


def _dot(a, b):
    return jnp.dot(a, b, preferred_element_type=F32)


def _dot_nt(a, b):
    return lax.dot_general(a, b, (((1,), (1,)), ((), ())), preferred_element_type=F32)


def _normmm_kernel(x_ref, g_ref, w_ref, cs_ref, o_ref, h_ref):
    @pl.when(pl.program_id(1) == 0)
    def _():
        x = x_ref[...]
        h_ref[...] = (x * _rms_scale(x) * g_ref[...]).astype(BF16)

    o_ref[...] = (_dot(h_ref[...], w_ref[...]) * cs_ref[...]).astype(o_ref.dtype)


def _norm_matmul(x, g, w, colscale, out_dtype, tm, tn):
    m, k = x.shape
    n = w.shape[1]
    return pl.pallas_call(
        _normmm_kernel,
        grid=(m // tm, n // tn),
        in_specs=[
            pl.BlockSpec((tm, k), lambda i, j: (i, 0)),
            pl.BlockSpec((1, k), lambda i, j: (0, 0)),
            pl.BlockSpec((k, tn), lambda i, j: (0, j)),
            pl.BlockSpec((1, tn), lambda i, j: (0, j)),
        ],
        out_specs=pl.BlockSpec((tm, tn), lambda i, j: (i, j)),
        out_shape=jax.ShapeDtypeStruct((m, n), out_dtype),
        scratch_shapes=[pltpu.VMEM((tm, k), BF16)],
        compiler_params=pltpu.CompilerParams(
            dimension_semantics=("parallel", "arbitrary"), vmem_limit_bytes=VMEM_LIMIT),
        name="norm_matmul",
    )(x, g.reshape(1, k), w, colscale.reshape(1, n))


def _mask_top(x, keep, fill):
    n = keep.shape[0]
    top = jnp.where(keep, x[:n], fill)
    return top if x.shape[0] == n else jnp.concatenate([top, x[n:]], axis=0)


def _walk_key_blocks(block, i, nd):
    for d in reversed(range(nd)):
        block(i * nd + d, d, True)

    def body(j, carry):
        block(i * nd - 1 - j, 0, False)
        return carry

    lax.fori_loop(0, i * nd, body, 0)


def _sb_kernel(q_ref, k_ref, v_ref, g_ref, u_ref, o_ref, acc_ref, carry_ref, *, tq):
    i = pl.program_id(2)
    nsub = tq // SB_SUB
    u = u_ref[...]
    row = lax.broadcasted_iota(jnp.int32, (SB_SUB, SB_SUB), 0)
    col = lax.broadcasted_iota(jnp.int32, (SB_SUB, SB_SUB), 1)
    causal = col < row
    acc_ref[...] = jnp.zeros_like(acc_ref)
    carry_ref[...] = jnp.zeros_like(carry_ref)

    def rows(r):
        return slice(r * SB_SUB, (r + 1) * SB_SUB)

    def per_sub_block(fn, x):
        return jnp.concatenate([fn(x[rows(r)]) for r in range(nsub)], axis=0)

    def step(o, diagonal):
        starts, zs = [], []
        for r in range(nsub):
            kb = i * nsub + r - o
            if not diagonal:
                carry_ref[rows(r), :] = jnp.where(kb >= 0, carry_ref[rows(r), :], SB_NO_KEYS_LEFT)
                kb = jnp.maximum(kb, 0)
            starts.append(pl.multiple_of(kb * SB_SUB, SB_SUB))
            zs.append(_dot_nt(q_ref[rows(r), :], k_ref[pl.ds(starts[r], SB_SUB), :]))
        z = jnp.concatenate(zs, axis=0)
        neg_log = jnp.maximum(z, jnp.log(1.0 + jnp.exp2(jnp.minimum(z, EXP2_CLAMP))) * LOG2_E)
        if diagonal:
            neg_log = per_sub_block(lambda t: jnp.where(causal, t, 0.0), neg_log)
        suffix = _dot(neg_log.astype(BF16), u)
        seen = carry_ref[...]
        a = jnp.exp2(z - suffix - jnp.concatenate([seen] * (SB_SUB // LANES), axis=1))
        if diagonal:
            a = per_sub_block(lambda t: jnp.where(causal, t, 0.0), a)
        a = a.astype(BF16)
        for r in range(nsub):
            acc_ref[rows(r), :] += _dot(a[rows(r)], v_ref[pl.ds(starts[r], SB_SUB), :])
        carry_ref[...] = seen + jnp.sum(neg_log, axis=-1, keepdims=True)

    step(0, True)

    def more(state):
        o, lightest = state
        return jnp.logical_and(o < (i + 1) * nsub, lightest < SB_DEAD_LOG2)

    def advance(state):
        step(state[0], False)
        return state[0] + 1, jnp.min(carry_ref[...])

    lax.while_loop(more, advance, (1, jnp.min(carry_ref[...])))
    o_ref[...] = (acc_ref[...] * _silu(g_ref[...].astype(F32))).astype(o_ref.dtype)


def _sb_attention(pa, batch, seq):
    tq = SB_TQ
    nq = seq // tq
    h0 = N_SB_HEADS
    u = jnp.tril(jnp.ones((SB_SUB, SB_SUB), F32)).astype(BF16)
    return pl.pallas_call(
        functools.partial(_sb_kernel, tq=tq),
        grid=(batch, N_SB_HEADS, nq),
        in_specs=[
            pl.BlockSpec((tq, HEAD_DIM), lambda b, h, i: (b * nq + i, h)),
            pl.BlockSpec((seq, HEAD_DIM), lambda b, h, i: (b, h0 + h)),
            pl.BlockSpec((seq, HEAD_DIM), lambda b, h, i: (b, 2 * h0 + h)),
            pl.BlockSpec((tq, HEAD_DIM), lambda b, h, i: (b * nq + i, 3 * h0 + h)),
            pl.BlockSpec((SB_SUB, SB_SUB), lambda b, h, i: (0, 0)),
        ],
        out_specs=pl.BlockSpec((tq, HEAD_DIM), lambda b, h, i: (b * nq + i, h)),
        out_shape=jax.ShapeDtypeStruct((batch * seq, SB_W), BF16),
        scratch_shapes=[pltpu.VMEM((tq, HEAD_DIM), F32), pltpu.VMEM((tq, LANES), F32)],
        compiler_params=pltpu.CompilerParams(
            dimension_semantics=("parallel", "parallel", "arbitrary"), vmem_limit_bytes=VMEM_LIMIT),
        name="sb_attention",
    )(pa, pa, pa, pa, u)


def _mla_kernel(q_ref, k_ref, v_ref, g_ref, o_ref, acc_ref, m_ref, den_ref, knorm_ref, *, tq, tk):
    i = pl.program_id(2)
    row = lax.broadcasted_iota(jnp.int32, (tk, tk), 0)
    col = lax.broadcasted_iota(jnp.int32, (tk, tk), 1)
    causal = col <= row
    nlane = tk // LANES

    @pl.when(i == 0)
    def _():
        def chunk(c, best):
            k = k_ref[pl.ds(pl.multiple_of(c * tk, tk), tk), :].astype(F32)
            return jnp.maximum(best, jnp.sum(k * k, axis=-1, keepdims=True))

        best = lax.fori_loop(0, k_ref.shape[0] // tk, chunk, jnp.zeros((tk, 1), F32))
        knorm_ref[...] = jnp.sqrt(jnp.broadcast_to(jnp.max(best, axis=0, keepdims=True), knorm_ref.shape))

    q = q_ref[...].astype(F32)
    bound = jnp.sqrt(jnp.sum(q * q, axis=-1, keepdims=True)) * knorm_ref[0:1, :]
    acc_ref[...] = jnp.zeros_like(acc_ref)
    den_ref[...] = jnp.zeros_like(den_ref)

    def scores(kb, d, diagonal):
        start = pl.multiple_of(kb * tk, tk)
        s = _dot_nt(q_ref[d * tk:, :], k_ref[pl.ds(start, tk), :])
        return _mask_top(s, causal, -jnp.inf) if diagonal else s

    def finish(den):
        o_ref[...] = (acc_ref[...] / den * _silu(g_ref[...].astype(F32))).astype(o_ref.dtype)

    def bounded():
        m_ref[...] = bound

        def block(kb, d, diagonal):
            r0 = d * tk
            s = scores(kb, d, diagonal)
            m = m_ref[r0:, :]
            ps = [jnp.exp2(s[:, c * LANES:(c + 1) * LANES] - m) for c in range(nlane)]
            den_ref[r0:, :] += functools.reduce(lambda a, b: a + b, ps)
            p = jnp.concatenate(ps, axis=1).astype(BF16)
            acc_ref[r0:, :] += _dot(p, v_ref[pl.ds(pl.multiple_of(kb * tk, tk), tk), :])

        _walk_key_blocks(block, i, tq // tk)
        finish(jnp.sum(den_ref[...], axis=-1, keepdims=True))

    def online():
        m_ref[...] = jnp.full_like(m_ref, -jnp.inf)

        def block(kb, d, diagonal):
            r0 = d * tk
            s = scores(kb, d, diagonal)
            m_old = m_ref[r0:, :]
            m_new = jnp.maximum(m_old, jnp.max(s, axis=-1, keepdims=True))
            alpha = jnp.exp2(m_old - m_new)
            p = jnp.concatenate([jnp.exp2(s[:, c * LANES:(c + 1) * LANES] - m_new) for c in range(nlane)], axis=1)
            den_ref[r0:, :] = alpha * den_ref[r0:, :] + jnp.sum(p, axis=-1, keepdims=True)
            pv = _dot(p.astype(BF16), v_ref[pl.ds(pl.multiple_of(kb * tk, tk), tk), :])
            acc_ref[r0:, :] = alpha * acc_ref[r0:, :] + pv
            m_ref[r0:, :] = m_new

        _walk_key_blocks(block, i, tq // tk)
        finish(den_ref[...])

    lax.cond(jnp.max(bound) <= MLA_SAFE_BOUND, bounded, online)


def _mla_attention(qx, kx, v, gq, batch, seq):
    tq, tk = MLA_TQ, MLA_TK
    nq = seq // tq
    return pl.pallas_call(
        functools.partial(_mla_kernel, tq=tq, tk=tk),
        grid=(batch, N_MLA_HEADS, nq),
        in_specs=[
            pl.BlockSpec((tq, QK_PAD), lambda b, h, i: (b * nq + i, h)),
            pl.BlockSpec((seq, QK_PAD), lambda b, h, i: (b, h)),
            pl.BlockSpec((seq, V_HEAD_DIM), lambda b, h, i: (b, h)),
            pl.BlockSpec((tq, V_HEAD_DIM), lambda b, h, i: (b * nq + i, h)),
        ],
        out_specs=pl.BlockSpec((tq, V_HEAD_DIM), lambda b, h, i: (b * nq + i, h)),
        out_shape=jax.ShapeDtypeStruct((batch * seq, MLA_W), BF16),
        scratch_shapes=[pltpu.VMEM((tq, V_HEAD_DIM), F32), pltpu.VMEM((tq, LANES), F32),
                        pltpu.VMEM((tq, LANES), F32), pltpu.VMEM((SUBLANES, LANES), F32)],
        compiler_params=pltpu.CompilerParams(
            dimension_semantics=("parallel", "parallel", "arbitrary"), vmem_limit_bytes=VMEM_LIMIT),
        name="mla_attention",
    )(qx, kx, v, gq)


def _out_kernel(main_ref, qm_ref, gm_ref, mkv_ref, gq_ref, gk_ref, x_ref, w_ref, o_ref, *, main_w):
    heads = []
    for h in range(N_MEM_HEADS):
        lo, hi = h * HEAD_DIM, (h + 1) * HEAD_DIM
        mk = mkv_ref[0, :, lo:hi]
        mk = (mk * _rms_scale(mk) * gk_ref[...]).astype(BF16)
        mv = mkv_ref[0, :, MEM_W + lo:MEM_W + hi].astype(BF16)
        q = qm_ref[:, lo:hi].astype(F32)
        q = (q * _rms_scale(q) * gq_ref[...]).astype(BF16)
        s = _dot_nt(q, mk) * HEAD_DIM ** -0.5
        p = jnp.exp(s - jnp.max(s, axis=-1, keepdims=True))
        mo = _dot(p.astype(BF16), mv) / jnp.sum(p, axis=-1, keepdims=True)
        heads.append((mo * _silu(gm_ref[:, lo:hi].astype(F32))).astype(BF16))
    acc = _dot(main_ref[...], w_ref[:main_w, :]) + _dot(jnp.concatenate(heads, axis=1), w_ref[main_w:, :])
    o_ref[...] = x_ref[...] + acc


def _mem_out_proj(main, side, qm_blk, gm_blk, mkv, g_q, g_k, x, w, seq):
    m, main_w = main.shape
    d = w.shape[1]
    mem_len = mkv.shape[1]
    tm = OUT_TM
    per_b = seq // tm
    return pl.pallas_call(
        functools.partial(_out_kernel, main_w=main_w),
        grid=(m // tm,),
        in_specs=[
            pl.BlockSpec((tm, main_w), lambda i: (i, 0)),
            pl.BlockSpec((tm, MEM_W), lambda i: (i, qm_blk)),
            pl.BlockSpec((tm, MEM_W), lambda i: (i, gm_blk)),
            pl.BlockSpec((1, mem_len, 2 * MEM_W), lambda i: (i // per_b, 0, 0)),
            pl.BlockSpec((1, HEAD_DIM), lambda i: (0, 0)),
            pl.BlockSpec((1, HEAD_DIM), lambda i: (0, 0)),
            pl.BlockSpec((tm, d), lambda i: (i, 0)),
            pl.BlockSpec((main_w + MEM_W, d), lambda i: (0, 0), pipeline_mode=pl.Buffered(1)),
        ],
        out_specs=pl.BlockSpec((tm, d), lambda i: (i, 0)),
        out_shape=jax.ShapeDtypeStruct((m, d), F32),
        compiler_params=pltpu.CompilerParams(
            dimension_semantics=("parallel",), vmem_limit_bytes=VMEM_LIMIT),
        name="mem_out_proj",
    )(main, side, side, mkv, g_q.reshape(1, HEAD_DIM), g_k.reshape(1, HEAD_DIM), x, w)


def _rope_tables(pos_ref, c_ref):
    ang = pos_ref[...].astype(F32) * c_ref[0:1, :]
    cosv, sinv = jnp.cos(ang), jnp.sin(ang)
    return cosv * c_ref[1:2, :], sinv * c_ref[2:3, :], sinv * c_ref[3:4, :]


def _rope(x, tables):
    c, s1, s2 = tables
    return x * c + pltpu.roll(x, LANES - QK_ROPE_DIM // 2, 1) * s1 + pltpu.roll(x, QK_ROPE_DIM // 2, 1) * s2


def _kv_kernel(x_ref, pos_ref, c_ref, gx_ref, wd_ref, gc_ref, wu_ref, gkn_ref, gkr_ref, kx_ref, v_ref):
    x = x_ref[...]
    h = (x * _rms_scale(x) * gx_ref[...]).astype(BF16)
    c = _dot(h, wd_ref[...])
    ckv = c[:, :KV_LORA_RANK]
    cn = (ckv * _rms_scale(ckv) * gc_ref[...]).astype(BF16)
    kv = _dot(cn, wu_ref[...])
    kr = c[:, KV_LORA_RANK:]
    kr = kr * _rms_scale(kr, QK_ROPE_DIM) * gkr_ref[...]
    k_rope = _rope(kr, _rope_tables(pos_ref, c_ref)).astype(BF16)
    for hd in range(N_MLA_HEADS):
        base = hd * QK_PAD
        kn = kv[:, base:base + QK_NOPE_DIM]
        kx_ref[:, base:base + QK_NOPE_DIM] = (kn * _rms_scale(kn) * gkn_ref[...]).astype(BF16)
        kx_ref[:, base + QK_NOPE_DIM:base + QK_PAD] = k_rope
        v_ref[:, hd * V_HEAD_DIM:(hd + 1) * V_HEAD_DIM] = kv[:, base + QK_NOPE_DIM:base + QK_PAD].astype(BF16)


def _q_kernel(x_ref, pos_ref, c_ref, gx_ref, win_ref, gl_ref, wuq_ref, gqn_ref, gqr_ref, qx_ref, side_ref):
    x = x_ref[...]
    h = (x * _rms_scale(x) * gx_ref[...]).astype(BF16)
    p = _dot(h, win_ref[...])
    side_ref[...] = p[:, Q_LORA_RANK:].astype(BF16)
    ql = p[:, :Q_LORA_RANK]
    qn = (ql * _rms_scale(ql) * gl_ref[...]).astype(BF16)
    q = _dot(qn, wuq_ref[...])
    tables = _rope_tables(pos_ref, c_ref)
    scale = (QK_NOPE_DIM + QK_ROPE_DIM) ** -0.5 * LOG2_E
    for hd in range(N_MLA_HEADS):
        base = hd * QK_PAD
        qnope = q[:, base:base + QK_NOPE_DIM]
        qx_ref[:, base:base + QK_NOPE_DIM] = (qnope * _rms_scale(qnope) * gqn_ref[...] * scale).astype(BF16)
        qr = q[:, base + QK_NOPE_DIM:base + QK_PAD]
        qr = qr * _rms_scale(qr, QK_ROPE_DIM) * gqr_ref[...]
        qx_ref[:, base + QK_NOPE_DIM:base + QK_PAD] = (_rope(qr, tables) * scale).astype(BF16)


def _row_call(body, x, pos, consts, small, outs, tm, name):
    m, d = x.shape
    resident = [pl.BlockSpec(a.shape, lambda i: (0, 0), pipeline_mode=pl.Buffered(1)) for a in (consts, *small)]
    return pl.pallas_call(
        body,
        grid=(m // tm,),
        in_specs=[pl.BlockSpec((tm, d), lambda i: (i, 0)), pl.BlockSpec((tm, 1), lambda i: (i, 0))] + resident,
        out_specs=[pl.BlockSpec((tm, w), lambda i: (i, 0)) for w in outs],
        out_shape=[jax.ShapeDtypeStruct((m, w), BF16) for w in outs],
        compiler_params=pltpu.CompilerParams(
            dimension_semantics=("parallel",), vmem_limit_bytes=VMEM_LIMIT),
        name=name,
    )(x, pos, consts, *small)


def _rope_consts():
    half = QK_ROPE_DIM // 2
    inv_freq = jnp.power(ROPE_THETA, -jnp.arange(0, QK_ROPE_DIM, 2, dtype=F32) / QK_ROPE_DIM)
    lane = np.arange(LANES)
    rows = jnp.zeros((8, LANES), F32)
    rows = rows.at[0, :QK_ROPE_DIM].set(jnp.concatenate([inv_freq, inv_freq]))
    rows = rows.at[1].set(jnp.asarray(lane < QK_ROPE_DIM, F32))
    rows = rows.at[2].set(jnp.asarray(-(lane < half).astype(np.float32)))
    rows = rows.at[3].set(jnp.asarray(((lane >= half) & (lane < QK_ROPE_DIM)).astype(np.float32)))
    return rows


def _pad_cols(a, width):
    return jnp.pad(a, [(0, 0)] * (a.ndim - 1) + [(0, width - a.shape[-1])])


def kernel(x, mem, positions, a_norm, a_w_in, a_w_out, kv_norm, w_dkv, g_ckv, w_ukv, g_k_nope, g_k_rope, b_norm, b_w_in, b_g_q_lat, b_w_uq, b_g_q_nope, b_g_q_rope, b_w_out, mem_norm, w_mem_kv, g_mem_q, g_mem_k):
    batch, seq, d = x.shape
    m = batch * seq
    mem_len = mem.shape[1]
    x2 = x.reshape(m, d)
    mem2 = mem.reshape(batch * mem_len, d)
    pos = positions.reshape(m, 1)
    consts = _rope_consts()
    row = lambda g: g.reshape(1, -1)

    def mem_kv(layer):
        w = w_mem_kv[layer].astype(BF16)
        mkv = _norm_matmul(mem2, mem_norm[layer], w, jnp.ones((2 * MEM_W,), F32), F32, *MEMKV_TILES)
        return mkv.reshape(batch, mem_len, 2 * MEM_W)

    a_in_w = a_w_in.shape[-1]
    qscale = jnp.ones((a_in_w,), F32).at[:SB_W].set(HEAD_DIM ** -0.5 * LOG2_E)
    pa = _norm_matmul(x2, a_norm[0], a_w_in[0].astype(BF16), qscale, BF16, INPROJ_TM, a_in_w // INPROJ_NSPLIT)
    sb = _sb_attention(pa, batch, seq)
    qm_blk = 4 * SB_W // MEM_W
    x2 = _mem_out_proj(sb, pa, qm_blk, qm_blk + 1, mem_kv(0), g_mem_q[0], g_mem_k[0],
                       x2, a_w_out[0].astype(BF16), seq)

    wd = _pad_cols(w_dkv, KV_LORA_RANK + LANES).astype(BF16)
    kx, v = _row_call(
        _kv_kernel, x2, pos, consts,
        (row(kv_norm), wd, row(g_ckv), w_ukv.astype(BF16), row(g_k_nope), _pad_cols(row(g_k_rope), LANES)),
        (N_MLA_HEADS * QK_PAD, MLA_W), KV_TM, "mla_kv_side")

    wuq = b_w_uq[0].reshape(Q_LORA_RANK, N_MLA_HEADS, QK_NOPE_DIM + QK_ROPE_DIM)
    wuq = _pad_cols(wuq, QK_PAD).reshape(Q_LORA_RANK, N_MLA_HEADS * QK_PAD).astype(BF16)
    qx, side = _row_call(
        _q_kernel, x2, pos, consts,
        (row(b_norm[0]), b_w_in[0].astype(BF16), row(b_g_q_lat[0]), wuq, row(b_g_q_nope[0]),
         _pad_cols(row(b_g_q_rope[0]), LANES)),
        (N_MLA_HEADS * QK_PAD, MLA_W + 2 * MEM_W), Q_TM, "mla_q_side")
    att = _mla_attention(qx, kx, v, side, batch, seq)
    qm_blk = MLA_W // MEM_W
    x2 = _mem_out_proj(att, side, qm_blk, qm_blk + 1, mem_kv(1), g_mem_q[1], g_mem_k[1],
                       x2, b_w_out[0].astype(BF16), seq)
    return x2.reshape(batch, seq, d)
```

```python
import functools
import math

import jax
import jax.numpy as jnp
import numpy as np
from jax import lax
from jax.experimental import pallas as pl
from jax.experimental.pallas import tpu as pltpu

F32 = jnp.float32
BF16 = jnp.bfloat16

HEAD_DIM = 128
N_SB_HEADS = 12
N_MEM_HEADS = 4
N_MLA_HEADS = 12
Q_LORA_RANK = 512
KV_LORA_RANK = 512
QK_NOPE_DIM = 128
QK_ROPE_DIM = 64
V_HEAD_DIM = 128
ROPE_THETA = 10000.0
EPS = 1e-6
SB_W = N_SB_HEADS * HEAD_DIM
MEM_W = N_MEM_HEADS * HEAD_DIM
MLA_W = N_MLA_HEADS * V_HEAD_DIM
LOG2_E = math.log2(math.e)
LANES = 128
QK_PAD = 2 * LANES
VMEM_LIMIT = 56 * 1024 * 1024

INPROJ_TM, INPROJ_NSPLIT = 512, 2
MEMKV_TILES = (256, 512)
OUT_TM = 512
KV_TM = 1024
Q_TM, Q_CHAINS = 512, 2
SB_TQ = 2048
MLA_TQ, MLA_TK = 2048, 512
SB_SUB = 2 * LANES
EXP2_CLAMP = 126.0
SB_DEAD_LOG2 = 150.0
SB_NO_KEYS_LEFT = 1e30
MLA_SAFE_BOUND = 60.0
MLA_SCALE = (QK_NOPE_DIM + QK_ROPE_DIM) ** -0.5 * LOG2_E
MLA_BOUND_SLACK = 1.02


def _rms_scale(x, width=None):
    ss = jnp.sum(x * x, axis=-1, keepdims=True)
    return lax.rsqrt(ss / (x.shape[-1] if width is None else width) + EPS)


def _silu(g):
    return g / (1.0 + jnp.exp(-g))


def _dot(a, b):
    return jnp.dot(a, b, preferred_element_type=F32)


def _dot_nt(a, b):
    return lax.dot_general(a, b, (((1,), (1,)), ((), ())), preferred_element_type=F32)


def _normmm_kernel(x_ref, g_ref, w_ref, cs_ref, o_ref, h_ref):
    @pl.when(pl.program_id(1) == 0)
    def _():
        x = x_ref[...]
        h_ref[...] = (x * _rms_scale(x) * g_ref[...]).astype(BF16)

    o_ref[...] = (_dot(h_ref[...], w_ref[...]) * cs_ref[...]).astype(o_ref.dtype)


def _norm_matmul(x, g, w, colscale, out_dtype, tm, tn):
    m, k = x.shape
    n = w.shape[1]
    return pl.pallas_call(
        _normmm_kernel,
        grid=(m // tm, n // tn),
        in_specs=[
            pl.BlockSpec((tm, k), lambda i, j: (i, 0)),
            pl.BlockSpec((1, k), lambda i, j: (0, 0)),
            pl.BlockSpec((k, tn), lambda i, j: (0, j)),
            pl.BlockSpec((1, tn), lambda i, j: (0, j)),
        ],
        out_specs=pl.BlockSpec((tm, tn), lambda i, j: (i, j)),
        out_shape=jax.ShapeDtypeStruct((m, n), out_dtype),
        scratch_shapes=[pltpu.VMEM((tm, k), BF16)],
        compiler_params=pltpu.CompilerParams(
            dimension_semantics=("parallel", "arbitrary"), vmem_limit_bytes=VMEM_LIMIT),
        name="norm_matmul",
    )(x, g.reshape(1, k), w, colscale.reshape(1, n))


def _mask_top(x, keep, fill):
    n = keep.shape[0]
    top = jnp.where(keep, x[:n], fill)
    return top if x.shape[0] == n else jnp.concatenate([top, x[n:]], axis=0)


def _walk_key_blocks(block, i, nd):
    for d in reversed(range(nd)):
        block(i * nd + d, d, True)

    def body(j, carry):
        block(i * nd - 1 - j, 0, False)
        return carry

    lax.fori_loop(0, i * nd, body, 0)


def _sb_kernel(q_ref, k_ref, v_ref, g_ref, u_ref, o_ref, acc_ref, carry_ref, *, tq):
    i = pl.program_id(2)
    nsub = tq // SB_SUB
    u = u_ref[...]
    row = lax.broadcasted_iota(jnp.int32, (SB_SUB, SB_SUB), 0)
    col = lax.broadcasted_iota(jnp.int32, (SB_SUB, SB_SUB), 1)
    causal = col < row
    acc_ref[...] = jnp.zeros_like(acc_ref)
    carry_ref[...] = jnp.zeros_like(carry_ref)

    def rows(r):
        return slice(r * SB_SUB, (r + 1) * SB_SUB)

    def per_sub_block(fn, x):
        return jnp.concatenate([fn(x[rows(r)]) for r in range(nsub)], axis=0)

    def step(o, diagonal):
        starts, zs = [], []
        for r in range(nsub):
            kb = i * nsub + r - o
            if not diagonal:
                carry_ref[rows(r), :] = jnp.where(kb >= 0, carry_ref[rows(r), :], SB_NO_KEYS_LEFT)
                kb = jnp.maximum(kb, 0)
            starts.append(pl.multiple_of(kb * SB_SUB, SB_SUB))
            zs.append(_dot_nt(q_ref[rows(r), :], k_ref[pl.ds(starts[r], SB_SUB), :]))
        z = jnp.concatenate(zs, axis=0)
        neg_log = jnp.maximum(z, jnp.log(1.0 + jnp.exp2(jnp.minimum(z, EXP2_CLAMP))) * LOG2_E)
        if diagonal:
            neg_log = per_sub_block(lambda t: jnp.where(causal, t, 0.0), neg_log)
        suffix = _dot(neg_log.astype(BF16), u)
        seen = carry_ref[...]
        a = jnp.exp2(z - suffix - jnp.concatenate([seen] * (SB_SUB // LANES), axis=1))
        if diagonal:
            a = per_sub_block(lambda t: jnp.where(causal, t, 0.0), a)
        a = a.astype(BF16)
        for r in range(nsub):
            acc_ref[rows(r), :] += _dot(a[rows(r)], v_ref[pl.ds(starts[r], SB_SUB), :])
        carry_ref[...] = seen + jnp.sum(neg_log, axis=-1, keepdims=True)

    step(0, True)

    def more(state):
        o, lightest = state
        return jnp.logical_and(o < (i + 1) * nsub, lightest < SB_DEAD_LOG2)

    def advance(state):
        step(state[0], False)
        return state[0] + 1, jnp.min(carry_ref[...])

    lax.while_loop(more, advance, (1, jnp.min(carry_ref[...])))
    o_ref[...] = (acc_ref[...] * _silu(g_ref[...].astype(F32))).astype(o_ref.dtype)


def _sb_attention(pa, batch, seq):
    tq = SB_TQ
    nq = seq // tq
    h0 = N_SB_HEADS
    u = jnp.tril(jnp.ones((SB_SUB, SB_SUB), F32)).astype(BF16)
    return pl.pallas_call(
        functools.partial(_sb_kernel, tq=tq),
        grid=(batch, N_SB_HEADS, nq),
        in_specs=[
            pl.BlockSpec((tq, HEAD_DIM), lambda b, h, i: (b * nq + i, h)),
            pl.BlockSpec((seq, HEAD_DIM), lambda b, h, i: (b, h0 + h)),
            pl.BlockSpec((seq, HEAD_DIM), lambda b, h, i: (b, 2 * h0 + h)),
            pl.BlockSpec((tq, HEAD_DIM), lambda b, h, i: (b * nq + i, 3 * h0 + h)),
            pl.BlockSpec((SB_SUB, SB_SUB), lambda b, h, i: (0, 0)),
        ],
        out_specs=pl.BlockSpec((tq, HEAD_DIM), lambda b, h, i: (b * nq + i, h)),
        out_shape=jax.ShapeDtypeStruct((batch * seq, SB_W), BF16),
        scratch_shapes=[pltpu.VMEM((tq, HEAD_DIM), F32), pltpu.VMEM((tq, LANES), F32)],
        compiler_params=pltpu.CompilerParams(
            dimension_semantics=("parallel", "parallel", "arbitrary"), vmem_limit_bytes=VMEM_LIMIT),
        name="sb_attention",
    )(pa, pa, pa, pa, u)


def _mla_kernel(bound_ref, q_ref, k_ref, v_ref, g_ref, o_ref, acc_ref, m_ref, den_ref, *, tq, tk):
    i = pl.program_id(2)
    row = lax.broadcasted_iota(jnp.int32, (tk, tk), 0)
    col = lax.broadcasted_iota(jnp.int32, (tk, tk), 1)
    causal = col <= row
    nlane = tk // LANES
    bound = bound_ref[0]
    acc_ref[...] = jnp.zeros_like(acc_ref)
    den_ref[...] = jnp.zeros_like(den_ref)

    def scores(kb, d, diagonal):
        start = pl.multiple_of(kb * tk, tk)
        s = _dot_nt(q_ref[d * tk:, :], k_ref[pl.ds(start, tk), :])
        return _mask_top(s, causal, -jnp.inf) if diagonal else s

    def finish(den):
        o_ref[...] = (acc_ref[...] / den * _silu(g_ref[...].astype(F32))).astype(o_ref.dtype)

    def bounded():
        def block(kb, d, diagonal):
            r0 = d * tk
            s = scores(kb, d, diagonal)
            ps = [jnp.exp2(s[:, c * LANES:(c + 1) * LANES] - bound) for c in range(nlane)]
            den_ref[r0:, :] += functools.reduce(lambda a, b: a + b, ps)
            p = jnp.concatenate(ps, axis=1).astype(BF16)
            acc_ref[r0:, :] += _dot(p, v_ref[pl.ds(pl.multiple_of(kb * tk, tk), tk), :])

        _walk_key_blocks(block, i, tq // tk)
        finish(jnp.sum(den_ref[...], axis=-1, keepdims=True))

    def online():
        m_ref[...] = jnp.full_like(m_ref, -jnp.inf)

        def block(kb, d, diagonal):
            r0 = d * tk
            s = scores(kb, d, diagonal)
            m_old = m_ref[r0:, :]
            m_new = jnp.maximum(m_old, jnp.max(s, axis=-1, keepdims=True))
            alpha = jnp.exp2(m_old - m_new)
            p = jnp.concatenate([jnp.exp2(s[:, c * LANES:(c + 1) * LANES] - m_new) for c in range(nlane)], axis=1)
            den_ref[r0:, :] = alpha * den_ref[r0:, :] + jnp.sum(p, axis=-1, keepdims=True)
            pv = _dot(p.astype(BF16), v_ref[pl.ds(pl.multiple_of(kb * tk, tk), tk), :])
            acc_ref[r0:, :] = alpha * acc_ref[r0:, :] + pv
            m_ref[r0:, :] = m_new

        _walk_key_blocks(block, i, tq // tk)
        finish(den_ref[...])

    lax.cond(bound <= MLA_SAFE_BOUND, bounded, online)


def _mla_score_bound(g_q_nope, g_q_rope, g_k_nope, g_k_rope):
    def norm(g_nope, g_rope):
        return jnp.sqrt(QK_NOPE_DIM * jnp.max(g_nope * g_nope) + QK_ROPE_DIM * jnp.max(g_rope * g_rope))

    return (MLA_BOUND_SLACK * MLA_SCALE * norm(g_q_nope, g_q_rope) * norm(g_k_nope, g_k_rope)).reshape(1)


def _mla_attention(bound, qx, kx, v, gq, batch, seq):
    tq, tk = MLA_TQ, MLA_TK
    nq = seq // tq
    return pl.pallas_call(
        functools.partial(_mla_kernel, tq=tq, tk=tk),
        grid=(batch, N_MLA_HEADS, nq),
        in_specs=[
            pl.BlockSpec(memory_space=pltpu.SMEM),
            pl.BlockSpec((tq, QK_PAD), lambda b, h, i: (b * nq + i, h)),
            pl.BlockSpec((seq, QK_PAD), lambda b, h, i: (b, h)),
            pl.BlockSpec((seq, V_HEAD_DIM), lambda b, h, i: (b, h)),
            pl.BlockSpec((tq, V_HEAD_DIM), lambda b, h, i: (b * nq + i, h)),
        ],
        out_specs=pl.BlockSpec((tq, V_HEAD_DIM), lambda b, h, i: (b * nq + i, h)),
        out_shape=jax.ShapeDtypeStruct((batch * seq, MLA_W), BF16),
        scratch_shapes=[pltpu.VMEM((tq, V_HEAD_DIM), F32), pltpu.VMEM((tq, LANES), F32),
                        pltpu.VMEM((tq, LANES), F32)],
        compiler_params=pltpu.CompilerParams(
            dimension_semantics=("parallel", "parallel", "arbitrary"), vmem_limit_bytes=VMEM_LIMIT),
        name="mla_attention",
    )(bound, qx, kx, v, gq)


def _out_kernel(main_ref, qm_ref, gm_ref, mkv_ref, gq_ref, gk_ref, x_ref, w_ref, o_ref, *, main_w):
    heads = []
    for h in range(N_MEM_HEADS):
        lo, hi = h * HEAD_DIM, (h + 1) * HEAD_DIM
        mk = mkv_ref[0, :, lo:hi]
        mk = (mk * _rms_scale(mk) * gk_ref[...]).astype(BF16)
        mv = mkv_ref[0, :, MEM_W + lo:MEM_W + hi].astype(BF16)
        q = qm_ref[:, lo:hi].astype(F32)
        q = (q * _rms_scale(q) * gq_ref[...]).astype(BF16)
        s = _dot_nt(q, mk) * HEAD_DIM ** -0.5
        p = jnp.exp(s - jnp.max(s, axis=-1, keepdims=True))
        mo = _dot(p.astype(BF16), mv) / jnp.sum(p, axis=-1, keepdims=True)
        heads.append((mo * _silu(gm_ref[:, lo:hi].astype(F32))).astype(BF16))
    acc = _dot(main_ref[...], w_ref[:main_w, :]) + _dot(jnp.concatenate(heads, axis=1), w_ref[main_w:, :])
    o_ref[...] = x_ref[...] + acc


def _mem_out_proj(main, side, qm_blk, gm_blk, mkv, g_q, g_k, x, w, seq):
    m, main_w = main.shape
    d = w.shape[1]
    mem_len = mkv.shape[1]
    tm = OUT_TM
    per_b = seq // tm
    return pl.pallas_call(
        functools.partial(_out_kernel, main_w=main_w),
        grid=(m // tm,),
        in_specs=[
            pl.BlockSpec((tm, main_w), lambda i: (i, 0)),
            pl.BlockSpec((tm, MEM_W), lambda i: (i, qm_blk)),
            pl.BlockSpec((tm, MEM_W), lambda i: (i, gm_blk)),
            pl.BlockSpec((1, mem_len, 2 * MEM_W), lambda i: (i // per_b, 0, 0)),
            pl.BlockSpec((1, HEAD_DIM), lambda i: (0, 0)),
            pl.BlockSpec((1, HEAD_DIM), lambda i: (0, 0)),
            pl.BlockSpec((tm, d), lambda i: (i, 0)),
            pl.BlockSpec((main_w + MEM_W, d), lambda i: (0, 0), pipeline_mode=pl.Buffered(1)),
        ],
        out_specs=pl.BlockSpec((tm, d), lambda i: (i, 0)),
        out_shape=jax.ShapeDtypeStruct((m, d), F32),
        compiler_params=pltpu.CompilerParams(
            dimension_semantics=("parallel",), vmem_limit_bytes=VMEM_LIMIT),
        name="mem_out_proj",
    )(main, side, side, mkv, g_q.reshape(1, HEAD_DIM), g_k.reshape(1, HEAD_DIM), x, w)


def _rope_tables(pos_ref, c_ref):
    ang = pos_ref[...].astype(F32) * c_ref[0:1, :]
    cosv, sinv = jnp.cos(ang), jnp.sin(ang)
    return cosv * c_ref[1:2, :], sinv * c_ref[2:3, :], sinv * c_ref[3:4, :]


def _rope(x, tables):
    c, s1, s2 = tables
    return x * c + pltpu.roll(x, LANES - QK_ROPE_DIM // 2, 1) * s1 + pltpu.roll(x, QK_ROPE_DIM // 2, 1) * s2


def _kv_kernel(x_ref, pos_ref, c_ref, gx_ref, wd_ref, gc_ref, wu_ref, gkn_ref, gkr_ref, kx_ref, v_ref):
    x = x_ref[...]
    h = (x * _rms_scale(x) * gx_ref[...]).astype(BF16)
    c = _dot(h, wd_ref[...])
    ckv = c[:, :KV_LORA_RANK]
    cn = (ckv * _rms_scale(ckv) * gc_ref[...]).astype(BF16)
    kv = _dot(cn, wu_ref[...])
    kr = c[:, KV_LORA_RANK:]
    kr = kr * _rms_scale(kr, QK_ROPE_DIM) * gkr_ref[...]
    k_rope = _rope(kr, _rope_tables(pos_ref, c_ref)).astype(BF16)
    for hd in range(N_MLA_HEADS):
        base = hd * QK_PAD
        kn = kv[:, base:base + QK_NOPE_DIM]
        kx_ref[:, base:base + QK_NOPE_DIM] = (kn * _rms_scale(kn) * gkn_ref[...]).astype(BF16)
        kx_ref[:, base + QK_NOPE_DIM:base + QK_PAD] = k_rope
        v_ref[:, hd * V_HEAD_DIM:(hd + 1) * V_HEAD_DIM] = kv[:, base + QK_NOPE_DIM:base + QK_PAD].astype(BF16)


def _q_kernel(x_ref, pos_ref, c_ref, gx_ref, win_ref, gl_ref, wuq_ref, gqn_ref, gqr_ref, qx_ref, side_ref):
    chain = x_ref.shape[0] // Q_CHAINS
    for c in range(Q_CHAINS):
        rs = slice(c * chain, (c + 1) * chain)
        x = x_ref[rs, :]
        h = (x * _rms_scale(x) * gx_ref[...]).astype(BF16)
        p = _dot(h, win_ref[...])
        side_ref[rs, :] = p[:, Q_LORA_RANK:].astype(BF16)
        ql = p[:, :Q_LORA_RANK]
        qn = (ql * _rms_scale(ql) * gl_ref[...]).astype(BF16)
        q = _dot(qn, wuq_ref[...])
        tables = _rope_tables(pos_ref.at[rs, :], c_ref)
        for hd in range(N_MLA_HEADS):
            base = hd * QK_PAD
            qnope = q[:, base:base + QK_NOPE_DIM]
            qx_ref[rs, base:base + QK_NOPE_DIM] = (qnope * _rms_scale(qnope) * gqn_ref[...] * MLA_SCALE).astype(BF16)
            qr = q[:, base + QK_NOPE_DIM:base + QK_PAD]
            qr = qr * _rms_scale(qr, QK_ROPE_DIM) * gqr_ref[...]
            qx_ref[rs, base + QK_NOPE_DIM:base + QK_PAD] = (_rope(qr, tables) * MLA_SCALE).astype(BF16)


def _row_call(body, x, pos, consts, small, outs, tm, name):
    m, d = x.shape
    resident = [pl.BlockSpec(a.shape, lambda i: (0, 0), pipeline_mode=pl.Buffered(1)) for a in (consts, *small)]
    return pl.pallas_call(
        body,
        grid=(m // tm,),
        in_specs=[pl.BlockSpec((tm, d), lambda i: (i, 0)), pl.BlockSpec((tm, 1), lambda i: (i, 0))] + resident,
        out_specs=[pl.BlockSpec((tm, w), lambda i: (i, 0)) for w in outs],
        out_shape=[jax.ShapeDtypeStruct((m, w), BF16) for w in outs],
        compiler_params=pltpu.CompilerParams(
            dimension_semantics=("parallel",), vmem_limit_bytes=VMEM_LIMIT),
        name=name,
    )(x, pos, consts, *small)


def _rope_consts():
    half = QK_ROPE_DIM // 2
    inv_freq = jnp.power(ROPE_THETA, -jnp.arange(0, QK_ROPE_DIM, 2, dtype=F32) / QK_ROPE_DIM)
    lane = np.arange(LANES)
    rows = jnp.zeros((8, LANES), F32)
    rows = rows.at[0, :QK_ROPE_DIM].set(jnp.concatenate([inv_freq, inv_freq]))
    rows = rows.at[1].set(jnp.asarray(lane < QK_ROPE_DIM, F32))
    rows = rows.at[2].set(jnp.asarray(-(lane < half).astype(np.float32)))
    rows = rows.at[3].set(jnp.asarray(((lane >= half) & (lane < QK_ROPE_DIM)).astype(np.float32)))
    return rows


def _pad_cols(a, width):
    return jnp.pad(a, [(0, 0)] * (a.ndim - 1) + [(0, width - a.shape[-1])])


def kernel(x, mem, positions, a_norm, a_w_in, a_w_out, kv_norm, w_dkv, g_ckv, w_ukv, g_k_nope, g_k_rope, b_norm, b_w_in, b_g_q_lat, b_w_uq, b_g_q_nope, b_g_q_rope, b_w_out, mem_norm, w_mem_kv, g_mem_q, g_mem_k):
    batch, seq, d = x.shape
    m = batch * seq
    mem_len = mem.shape[1]
    x2 = x.reshape(m, d)
    mem2 = mem.reshape(batch * mem_len, d)
    pos = positions.reshape(m, 1)
    consts = _rope_consts()
    row = lambda g: g.reshape(1, -1)

    def mem_kv(layer):
        w = w_mem_kv[layer].astype(BF16)
        mkv = _norm_matmul(mem2, mem_norm[layer], w, jnp.ones((2 * MEM_W,), F32), F32, *MEMKV_TILES)
        return mkv.reshape(batch, mem_len, 2 * MEM_W)

    a_in_w = a_w_in.shape[-1]
    qscale = jnp.ones((a_in_w,), F32).at[:SB_W].set(HEAD_DIM ** -0.5 * LOG2_E)
    pa = _norm_matmul(x2, a_norm[0], a_w_in[0].astype(BF16), qscale, BF16, INPROJ_TM, a_in_w // INPROJ_NSPLIT)
    sb = _sb_attention(pa, batch, seq)
    qm_blk = 4 * SB_W // MEM_W
    x2 = _mem_out_proj(sb, pa, qm_blk, qm_blk + 1, mem_kv(0), g_mem_q[0], g_mem_k[0],
                       x2, a_w_out[0].astype(BF16), seq)

    wd = _pad_cols(w_dkv, KV_LORA_RANK + LANES).astype(BF16)
    kx, v = _row_call(
        _kv_kernel, x2, pos, consts,
        (row(kv_norm), wd, row(g_ckv), w_ukv.astype(BF16), row(g_k_nope), _pad_cols(row(g_k_rope), LANES)),
        (N_MLA_HEADS * QK_PAD, MLA_W), KV_TM, "mla_kv_side")

    wuq = b_w_uq[0].reshape(Q_LORA_RANK, N_MLA_HEADS, QK_NOPE_DIM + QK_ROPE_DIM)
    wuq = _pad_cols(wuq, QK_PAD).reshape(Q_LORA_RANK, N_MLA_HEADS * QK_PAD).astype(BF16)
    qx, side = _row_call(
        _q_kernel, x2, pos, consts,
        (row(b_norm[0]), b_w_in[0].astype(BF16), row(b_g_q_lat[0]), wuq, row(b_g_q_nope[0]),
         _pad_cols(row(b_g_q_rope[0]), LANES)),
        (N_MLA_HEADS * QK_PAD, MLA_W + 2 * MEM_W), Q_TM, "mla_q_side")
    bound = _mla_score_bound(b_g_q_nope[0], b_g_q_rope[0], g_k_nope, g_k_rope)
    att = _mla_attention(bound, qx, kx, v, side, batch, seq)
    qm_blk = MLA_W // MEM_W
    x2 = _mem_out_proj(att, side, qm_blk, qm_blk + 1, mem_kv(1), g_mem_q[1], g_mem_k[1],
                       x2, b_w_out[0].astype(BF16), seq)
    return x2.reshape(batch, seq, d)
```

```python
import functools
import math

import jax
import jax.numpy as jnp
import numpy as np
from jax import lax
from jax.experimental import pallas as pl
from jax.experimental.pallas import tpu as pltpu

F32 = jnp.float32
BF16 = jnp.bfloat16

HEAD_DIM = 128
N_SB_HEADS = 12
N_MEM_HEADS = 4
N_MLA_HEADS = 12
Q_LORA_RANK = 512
KV_LORA_RANK = 512
QK_NOPE_DIM = 128
QK_ROPE_DIM = 64
V_HEAD_DIM = 128
ROPE_THETA = 10000.0
EPS = 1e-6
SB_W = N_SB_HEADS * HEAD_DIM
MEM_W = N_MEM_HEADS * HEAD_DIM
MLA_W = N_MLA_HEADS * V_HEAD_DIM
LOG2_E = math.log2(math.e)
LANES = 128
QK_PAD = 2 * LANES
VMEM_LIMIT = 56 * 1024 * 1024

INPROJ_TM, INPROJ_NSPLIT = 512, 2
MEMKV_TILES = (256, 512)
OUT_TM = 512
KV_TM = 1024
Q_TM, Q_CHAINS = 512, 2
SB_TQ = 4096
MLA_TQ, MLA_TK = 2048, 512
SB_SUB = 2 * LANES
EXP2_CLAMP = 126.0
SB_DEAD_LOG2 = 150.0
SB_NO_KEYS_LEFT = 1e30
MLA_SAFE_BOUND = 60.0
MLA_SCALE = (QK_NOPE_DIM + QK_ROPE_DIM) ** -0.5 * LOG2_E
MLA_BOUND_SLACK = 1.02


def _rms_scale(x, width=None):
    ss = jnp.sum(x * x, axis=-1, keepdims=True)
    return lax.rsqrt(ss / (x.shape[-1] if width is None else width) + EPS)


def _silu(g):
    return g / (1.0 + jnp.exp(-g))


def _dot(a, b):
    return jnp.dot(a, b, preferred_element_type=F32)


def _dot_nt(a, b):
    return lax.dot_general(a, b, (((1,), (1,)), ((), ())), preferred_element_type=F32)


def _normmm_kernel(x_ref, g_ref, w_ref, cs_ref, o_ref, h_ref):
    @pl.when(pl.program_id(1) == 0)
    def _():
        x = x_ref[...]
        h_ref[...] = (x * _rms_scale(x) * g_ref[...]).astype(BF16)

    o_ref[...] = (_dot(h_ref[...], w_ref[...]) * cs_ref[...]).astype(o_ref.dtype)


def _norm_matmul(x, g, w, colscale, out_dtype, tm, tn):
    m, k = x.shape
    n = w.shape[1]
    return pl.pallas_call(
        _normmm_kernel,
        grid=(m // tm, n // tn),
        in_specs=[
            pl.BlockSpec((tm, k), lambda i, j: (i, 0)),
            pl.BlockSpec((1, k), lambda i, j: (0, 0)),
            pl.BlockSpec((k, tn), lambda i, j: (0, j)),
            pl.BlockSpec((1, tn), lambda i, j: (0, j)),
        ],
        out_specs=pl.BlockSpec((tm, tn), lambda i, j: (i, j)),
        out_shape=jax.ShapeDtypeStruct((m, n), out_dtype),
        scratch_shapes=[pltpu.VMEM((tm, k), BF16)],
        compiler_params=pltpu.CompilerParams(
            dimension_semantics=("parallel", "arbitrary"), vmem_limit_bytes=VMEM_LIMIT),
        name="norm_matmul",
    )(x, g.reshape(1, k), w, colscale.reshape(1, n))


def _mask_top(x, keep, fill):
    n = keep.shape[0]
    top = jnp.where(keep, x[:n], fill)
    return top if x.shape[0] == n else jnp.concatenate([top, x[n:]], axis=0)


def _walk_key_blocks(block, i, nd, unroll=1):
    assert nd % unroll == 0
    for d in reversed(range(nd)):
        block(i * nd + d, d, True)

    def body(j, carry):
        for u in range(unroll):
            block(i * nd - 1 - j * unroll - u, 0, False)
        return carry

    lax.fori_loop(0, i * (nd // unroll), body, 0)


def _sb_kernel(q_ref, k_ref, v_ref, g_ref, u_ref, o_ref, acc_ref, carry_ref, *, tq):
    i = pl.program_id(2)
    nsub = tq // SB_SUB
    u = u_ref[...]
    row = lax.broadcasted_iota(jnp.int32, (SB_SUB, SB_SUB), 0)
    col = lax.broadcasted_iota(jnp.int32, (SB_SUB, SB_SUB), 1)
    causal = col < row
    acc_ref[...] = jnp.zeros_like(acc_ref)
    carry_ref[...] = jnp.zeros_like(carry_ref)

    def rows(r):
        return slice(r * SB_SUB, (r + 1) * SB_SUB)

    def per_sub_block(fn, x):
        return jnp.concatenate([fn(x[rows(r)]) for r in range(nsub)], axis=0)

    def step(o, diagonal):
        starts, zs = [], []
        for r in range(nsub):
            kb = i * nsub + r - o
            if not diagonal:
                carry_ref[rows(r), :] = jnp.where(kb >= 0, carry_ref[rows(r), :], SB_NO_KEYS_LEFT)
                kb = jnp.maximum(kb, 0)
            starts.append(pl.multiple_of(kb * SB_SUB, SB_SUB))
            zs.append(_dot_nt(q_ref[rows(r), :], k_ref[pl.ds(starts[r], SB_SUB), :]))
        z = jnp.concatenate(zs, axis=0)
        neg_log = jnp.maximum(z, jnp.log(1.0 + jnp.exp2(jnp.minimum(z, EXP2_CLAMP))) * LOG2_E)
        if diagonal:
            neg_log = per_sub_block(lambda t: jnp.where(causal, t, 0.0), neg_log)
        suffix = _dot(neg_log.astype(BF16), u)
        seen = carry_ref[...]
        a = jnp.exp2(z - suffix - jnp.concatenate([seen] * (SB_SUB // LANES), axis=1))
        if diagonal:
            a = per_sub_block(lambda t: jnp.where(causal, t, 0.0), a)
        a = a.astype(BF16)
        for r in range(nsub):
            acc_ref[rows(r), :] += _dot(a[rows(r)], v_ref[pl.ds(starts[r], SB_SUB), :])
        carry_ref[...] = seen + jnp.sum(neg_log, axis=-1, keepdims=True)

    step(0, True)

    def more(state):
        o, lightest = state
        return jnp.logical_and(o < (i + 1) * nsub, lightest < SB_DEAD_LOG2)

    def advance(state):
        step(state[0], False)
        return state[0] + 1, jnp.min(carry_ref[...])

    lax.while_loop(more, advance, (1, jnp.min(carry_ref[...])))
    o_ref[...] = (acc_ref[...] * _silu(g_ref[...].astype(F32))).astype(o_ref.dtype)


def _sb_attention(pa, batch, seq):
    tq = SB_TQ
    nq = seq // tq
    h0 = N_SB_HEADS
    u = jnp.tril(jnp.ones((SB_SUB, SB_SUB), F32)).astype(BF16)
    return pl.pallas_call(
        functools.partial(_sb_kernel, tq=tq),
        grid=(batch, N_SB_HEADS, nq),
        in_specs=[
            pl.BlockSpec((tq, HEAD_DIM), lambda b, h, i: (b * nq + i, h)),
            pl.BlockSpec((seq, HEAD_DIM), lambda b, h, i: (b, h0 + h)),
            pl.BlockSpec((seq, HEAD_DIM), lambda b, h, i: (b, 2 * h0 + h)),
            pl.BlockSpec((tq, HEAD_DIM), lambda b, h, i: (b * nq + i, 3 * h0 + h)),
            pl.BlockSpec((SB_SUB, SB_SUB), lambda b, h, i: (0, 0)),
        ],
        out_specs=pl.BlockSpec((tq, HEAD_DIM), lambda b, h, i: (b * nq + i, h)),
        out_shape=jax.ShapeDtypeStruct((batch * seq, SB_W), BF16),
        scratch_shapes=[pltpu.VMEM((tq, HEAD_DIM), F32), pltpu.VMEM((tq, LANES), F32)],
        compiler_params=pltpu.CompilerParams(
            dimension_semantics=("parallel", "parallel", "arbitrary"), vmem_limit_bytes=VMEM_LIMIT),
        name="sb_attention",
    )(pa, pa, pa, pa, u)


def _mla_kernel(bound_ref, q_ref, k_ref, v_ref, g_ref, o_ref, acc_ref, m_ref, den_ref, *, tq, tk):
    i = pl.program_id(2)
    row = lax.broadcasted_iota(jnp.int32, (tk, tk), 0)
    col = lax.broadcasted_iota(jnp.int32, (tk, tk), 1)
    causal = col <= row
    nlane = tk // LANES
    bound = bound_ref[0]
    acc_ref[...] = jnp.zeros_like(acc_ref)
    den_ref[...] = jnp.zeros_like(den_ref)

    def scores(kb, d, diagonal):
        start = pl.multiple_of(kb * tk, tk)
        s = _dot_nt(q_ref[d * tk:, :], k_ref[pl.ds(start, tk), :])
        return _mask_top(s, causal, -jnp.inf) if diagonal else s

    def finish(den):
        o_ref[...] = (acc_ref[...] / den * _silu(g_ref[...].astype(F32))).astype(o_ref.dtype)

    def bounded():
        def block(kb, d, diagonal):
            r0 = d * tk
            s = scores(kb, d, diagonal)
            ps = [jnp.exp2(s[:, c * LANES:(c + 1) * LANES] - bound) for c in range(nlane)]
            den_ref[r0:, :] += functools.reduce(lambda a, b: a + b, ps)
            p = jnp.concatenate(ps, axis=1).astype(BF16)
            acc_ref[r0:, :] += _dot(p, v_ref[pl.ds(pl.multiple_of(kb * tk, tk), tk), :])

        _walk_key_blocks(block, i, tq // tk, unroll=tq // tk)
        finish(jnp.sum(den_ref[...], axis=-1, keepdims=True))

    def online():
        m_ref[...] = jnp.full_like(m_ref, -jnp.inf)

        def block(kb, d, diagonal):
            r0 = d * tk
            s = scores(kb, d, diagonal)
            m_old = m_ref[r0:, :]
            m_new = jnp.maximum(m_old, jnp.max(s, axis=-1, keepdims=True))
            alpha = jnp.exp2(m_old - m_new)
            p = jnp.concatenate([jnp.exp2(s[:, c * LANES:(c + 1) * LANES] - m_new) for c in range(nlane)], axis=1)
            den_ref[r0:, :] = alpha * den_ref[r0:, :] + jnp.sum(p, axis=-1, keepdims=True)
            pv = _dot(p.astype(BF16), v_ref[pl.ds(pl.multiple_of(kb * tk, tk), tk), :])
            acc_ref[r0:, :] = alpha * acc_ref[r0:, :] + pv
            m_ref[r0:, :] = m_new

        _walk_key_blocks(block, i, tq // tk)
        finish(den_ref[...])

    lax.cond(bound <= MLA_SAFE_BOUND, bounded, online)


def _mla_score_bound(g_q_nope, g_q_rope, g_k_nope, g_k_rope):
    def norm(g_nope, g_rope):
        return jnp.sqrt(QK_NOPE_DIM * jnp.max(g_nope * g_nope) + QK_ROPE_DIM * jnp.max(g_rope * g_rope))

    return (MLA_BOUND_SLACK * MLA_SCALE * norm(g_q_nope, g_q_rope) * norm(g_k_nope, g_k_rope)).reshape(1)


def _mla_attention(bound, qx, kx, v, gq, batch, seq):
    tq, tk = MLA_TQ, MLA_TK
    nq = seq // tq
    return pl.pallas_call(
        functools.partial(_mla_kernel, tq=tq, tk=tk),
        grid=(batch, N_MLA_HEADS, nq),
        in_specs=[
            pl.BlockSpec(memory_space=pltpu.SMEM),
            pl.BlockSpec((tq, QK_PAD), lambda b, h, i: (b * nq + i, h)),
            pl.BlockSpec((seq, QK_PAD), lambda b, h, i: (b, h)),
            pl.BlockSpec((seq, V_HEAD_DIM), lambda b, h, i: (b, h)),
            pl.BlockSpec((tq, V_HEAD_DIM), lambda b, h, i: (b * nq + i, h)),
        ],
        out_specs=pl.BlockSpec((tq, V_HEAD_DIM), lambda b, h, i: (b * nq + i, h)),
        out_shape=jax.ShapeDtypeStruct((batch * seq, MLA_W), BF16),
        scratch_shapes=[pltpu.VMEM((tq, V_HEAD_DIM), F32), pltpu.VMEM((tq, LANES), F32),
                        pltpu.VMEM((tq, LANES), F32)],
        compiler_params=pltpu.CompilerParams(
            dimension_semantics=("parallel", "parallel", "arbitrary"), vmem_limit_bytes=VMEM_LIMIT),
        name="mla_attention",
    )(bound, qx, kx, v, gq)


def _out_kernel(main_ref, qm_ref, gm_ref, mkv_ref, gq_ref, gk_ref, x_ref, w_ref, o_ref, *, main_w):
    heads = []
    for h in range(N_MEM_HEADS):
        lo, hi = h * HEAD_DIM, (h + 1) * HEAD_DIM
        mk = mkv_ref[0, :, lo:hi]
        mk = (mk * _rms_scale(mk) * gk_ref[...]).astype(BF16)
        mv = mkv_ref[0, :, MEM_W + lo:MEM_W + hi].astype(BF16)
        q = qm_ref[:, lo:hi].astype(F32)
        q = (q * _rms_scale(q) * gq_ref[...]).astype(BF16)
        s = _dot_nt(q, mk) * HEAD_DIM ** -0.5
        p = jnp.exp(s - jnp.max(s, axis=-1, keepdims=True))
        mo = _dot(p.astype(BF16), mv) / jnp.sum(p, axis=-1, keepdims=True)
        heads.append((mo * _silu(gm_ref[:, lo:hi].astype(F32))).astype(BF16))
    acc = _dot(main_ref[...], w_ref[:main_w, :]) + _dot(jnp.concatenate(heads, axis=1), w_ref[main_w:, :])
    o_ref[...] = x_ref[...] + acc


def _mem_out_proj(main, side, qm_blk, gm_blk, mkv, g_q, g_k, x, w, seq):
    m, main_w = main.shape
    d = w.shape[1]
    mem_len = mkv.shape[1]
    tm = OUT_TM
    per_b = seq // tm
    return pl.pallas_call(
        functools.partial(_out_kernel, main_w=main_w),
        grid=(m // tm,),
        in_specs=[
            pl.BlockSpec((tm, main_w), lambda i: (i, 0)),
            pl.BlockSpec((tm, MEM_W), lambda i: (i, qm_blk)),
            pl.BlockSpec((tm, MEM_W), lambda i: (i, gm_blk)),
            pl.BlockSpec((1, mem_len, 2 * MEM_W), lambda i: (i // per_b, 0, 0)),
            pl.BlockSpec((1, HEAD_DIM), lambda i: (0, 0)),
            pl.BlockSpec((1, HEAD_DIM), lambda i: (0, 0)),
            pl.BlockSpec((tm, d), lambda i: (i, 0)),
            pl.BlockSpec((main_w + MEM_W, d), lambda i: (0, 0), pipeline_mode=pl.Buffered(1)),
        ],
        out_specs=pl.BlockSpec((tm, d), lambda i: (i, 0)),
        out_shape=jax.ShapeDtypeStruct((m, d), F32),
        compiler_params=pltpu.CompilerParams(
            dimension_semantics=("parallel",), vmem_limit_bytes=VMEM_LIMIT),
        name="mem_out_proj",
    )(main, side, side, mkv, g_q.reshape(1, HEAD_DIM), g_k.reshape(1, HEAD_DIM), x, w)


def _rope_tables(pos_ref, c_ref):
    ang = pos_ref[...].astype(F32) * c_ref[0:1, :]
    cosv, sinv = jnp.cos(ang), jnp.sin(ang)
    return cosv * c_ref[1:2, :], sinv * c_ref[2:3, :], sinv * c_ref[3:4, :]


def _rope(x, tables):
    c, s1, s2 = tables
    return x * c + pltpu.roll(x, LANES - QK_ROPE_DIM // 2, 1) * s1 + pltpu.roll(x, QK_ROPE_DIM // 2, 1) * s2


def _kv_kernel(x_ref, pos_ref, c_ref, gx_ref, wd_ref, gc_ref, wu_ref, gkn_ref, gkr_ref, kx_ref, v_ref):
    x = x_ref[...]
    h = (x * _rms_scale(x) * gx_ref[...]).astype(BF16)
    c = _dot(h, wd_ref[...])
    ckv = c[:, :KV_LORA_RANK]
    cn = (ckv * _rms_scale(ckv) * gc_ref[...]).astype(BF16)
    kv = _dot(cn, wu_ref[...])
    kr = c[:, KV_LORA_RANK:]
    kr = kr * _rms_scale(kr, QK_ROPE_DIM) * gkr_ref[...]
    k_rope = _rope(kr, _rope_tables(pos_ref, c_ref)).astype(BF16)
    for hd in range(N_MLA_HEADS):
        base = hd * QK_PAD
        kn = kv[:, base:base + QK_NOPE_DIM]
        kx_ref[:, base:base + QK_NOPE_DIM] = (kn * _rms_scale(kn) * gkn_ref[...]).astype(BF16)
        kx_ref[:, base + QK_NOPE_DIM:base + QK_PAD] = k_rope
        v_ref[:, hd * V_HEAD_DIM:(hd + 1) * V_HEAD_DIM] = kv[:, base + QK_NOPE_DIM:base + QK_PAD].astype(BF16)


def _q_kernel(x_ref, pos_ref, c_ref, gx_ref, win_ref, gl_ref, wuq_ref, gqn_ref, gqr_ref, qx_ref, side_ref):
    chain = x_ref.shape[0] // Q_CHAINS
    for c in range(Q_CHAINS):
        rs = slice(c * chain, (c + 1) * chain)
        x = x_ref[rs, :]
        h = (x * _rms_scale(x) * gx_ref[...]).astype(BF16)
        p = _dot(h, win_ref[...])
        side_ref[rs, :] = p[:, Q_LORA_RANK:].astype(BF16)
        ql = p[:, :Q_LORA_RANK]
        qn = (ql * _rms_scale(ql) * gl_ref[...]).astype(BF16)
        q = _dot(qn, wuq_ref[...])
        tables = _rope_tables(pos_ref.at[rs, :], c_ref)
        for hd in range(N_MLA_HEADS):
            base = hd * QK_PAD
            qnope = q[:, base:base + QK_NOPE_DIM]
            qx_ref[rs, base:base + QK_NOPE_DIM] = (qnope * _rms_scale(qnope) * gqn_ref[...] * MLA_SCALE).astype(BF16)
            qr = q[:, base + QK_NOPE_DIM:base + QK_PAD]
            qr = qr * _rms_scale(qr, QK_ROPE_DIM) * gqr_ref[...]
            qx_ref[rs, base + QK_NOPE_DIM:base + QK_PAD] = (_rope(qr, tables) * MLA_SCALE).astype(BF16)


def _row_call(body, x, pos, consts, small, outs, tm, name):
    m, d = x.shape
    resident = [pl.BlockSpec(a.shape, lambda i: (0, 0), pipeline_mode=pl.Buffered(1)) for a in (consts, *small)]
    return pl.pallas_call(
        body,
        grid=(m // tm,),
        in_specs=[pl.BlockSpec((tm, d), lambda i: (i, 0)), pl.BlockSpec((tm, 1), lambda i: (i, 0))] + resident,
        out_specs=[pl.BlockSpec((tm, w), lambda i: (i, 0)) for w in outs],
        out_shape=[jax.ShapeDtypeStruct((m, w), BF16) for w in outs],
        compiler_params=pltpu.CompilerParams(
            dimension_semantics=("parallel",), vmem_limit_bytes=VMEM_LIMIT),
        name=name,
    )(x, pos, consts, *small)


def _rope_consts():
    half = QK_ROPE_DIM // 2
    inv_freq = jnp.power(ROPE_THETA, -jnp.arange(0, QK_ROPE_DIM, 2, dtype=F32) / QK_ROPE_DIM)
    lane = np.arange(LANES)
    rows = jnp.zeros((8, LANES), F32)
    rows = rows.at[0, :QK_ROPE_DIM].set(jnp.concatenate([inv_freq, inv_freq]))
    rows = rows.at[1].set(jnp.asarray(lane < QK_ROPE_DIM, F32))
    rows = rows.at[2].set(jnp.asarray(-(lane < half).astype(np.float32)))
    rows = rows.at[3].set(jnp.asarray(((lane >= half) & (lane < QK_ROPE_DIM)).astype(np.float32)))
    return rows


def _pad_cols(a, width):
    return jnp.pad(a, [(0, 0)] * (a.ndim - 1) + [(0, width - a.shape[-1])])


def kernel(x, mem, positions, a_norm, a_w_in, a_w_out, kv_norm, w_dkv, g_ckv, w_ukv, g_k_nope, g_k_rope, b_norm, b_w_in, b_g_q_lat, b_w_uq, b_g_q_nope, b_g_q_rope, b_w_out, mem_norm, w_mem_kv, g_mem_q, g_mem_k):
    batch, seq, d = x.shape
    m = batch * seq
    mem_len = mem.shape[1]
    x2 = x.reshape(m, d)
    mem2 = mem.reshape(batch * mem_len, d)
    pos = positions.reshape(m, 1)
    consts = _rope_consts()
    row = lambda g: g.reshape(1, -1)

    def mem_kv(layer):
        w = w_mem_kv[layer].astype(BF16)
        mkv = _norm_matmul(mem2, mem_norm[layer], w, jnp.ones((2 * MEM_W,), F32), F32, *MEMKV_TILES)
        return mkv.reshape(batch, mem_len, 2 * MEM_W)

    a_in_w = a_w_in.shape[-1]
    qscale = jnp.ones((a_in_w,), F32).at[:SB_W].set(HEAD_DIM ** -0.5 * LOG2_E)
    pa = _norm_matmul(x2, a_norm[0], a_w_in[0].astype(BF16), qscale, BF16, INPROJ_TM, a_in_w // INPROJ_NSPLIT)
    sb = _sb_attention(pa, batch, seq)
    qm_blk = 4 * SB_W // MEM_W
    x2 = _mem_out_proj(sb, pa, qm_blk, qm_blk + 1, mem_kv(0), g_mem_q[0], g_mem_k[0],
                       x2, a_w_out[0].astype(BF16), seq)

    wd = _pad_cols(w_dkv, KV_LORA_RANK + LANES).astype(BF16)
    kx, v = _row_call(
        _kv_kernel, x2, pos, consts,
        (row(kv_norm), wd, row(g_ckv), w_ukv.astype(BF16), row(g_k_nope), _pad_cols(row(g_k_rope), LANES)),
        (N_MLA_HEADS * QK_PAD, MLA_W), KV_TM, "mla_kv_side")

    wuq = b_w_uq[0].reshape(Q_LORA_RANK, N_MLA_HEADS, QK_NOPE_DIM + QK_ROPE_DIM)
    wuq = _pad_cols(wuq, QK_PAD).reshape(Q_LORA_RANK, N_MLA_HEADS * QK_PAD).astype(BF16)
    qx, side = _row_call(
        _q_kernel, x2, pos, consts,
        (row(b_norm[0]), b_w_in[0].astype(BF16), row(b_g_q_lat[0]), wuq, row(b_g_q_nope[0]),
         _pad_cols(row(b_g_q_rope[0]), LANES)),
        (N_MLA_HEADS * QK_PAD, MLA_W + 2 * MEM_W), Q_TM, "mla_q_side")
    bound = _mla_score_bound(b_g_q_nope[0], b_g_q_rope[0], g_k_nope, g_k_rope)
    att = _mla_attention(bound, qx, kx, v, side, batch, seq)
    qm_blk = MLA_W // MEM_W
    x2 = _mem_out_proj(att, side, qm_blk, qm_blk + 1, mem_kv(1), g_mem_q[1], g_mem_k[1],
                       x2, b_w_out[0].astype(BF16), seq)
    return x2.reshape(batch, seq, d)
```

```python
import functools
import math

import jax
import jax.numpy as jnp
import numpy as np
from jax import lax
from jax.experimental import pallas as pl
from jax.experimental.pallas import tpu as pltpu

F32 = jnp.float32
BF16 = jnp.bfloat16

HEAD_DIM = 128
N_SB_HEADS = 12
N_MEM_HEADS = 4
N_MLA_HEADS = 12
Q_LORA_RANK = 512
KV_LORA_RANK = 512
QK_NOPE_DIM = 128
QK_ROPE_DIM = 64
V_HEAD_DIM = 128
ROPE_THETA = 10000.0
EPS = 1e-6
SB_W = N_SB_HEADS * HEAD_DIM
MEM_W = N_MEM_HEADS * HEAD_DIM
MLA_W = N_MLA_HEADS * V_HEAD_DIM
LOG2_E = math.log2(math.e)
LANES = 128
QK_PAD = 2 * LANES
VMEM_LIMIT = 56 * 1024 * 1024

INPROJ_TM, INPROJ_NSPLIT = 512, 2
MEMKV_TILES = (256, 512)
OUT_TM = 512
KV_TM = 1024
Q_TM = 512
SB_TQ = 4096
MLA_TQ, MLA_TK = 2048, 512
SB_SUB = 2 * LANES
EXP2_CLAMP = 126.0
SB_DEAD_LOG2 = 150.0
SB_NO_KEYS_LEFT = 1e30
MLA_SAFE_BOUND = 60.0
MLA_SCALE = (QK_NOPE_DIM + QK_ROPE_DIM) ** -0.5 * LOG2_E
MLA_BOUND_SLACK = 1.02


def _rms_scale(x, width=None):
    ss = jnp.sum(x * x, axis=-1, keepdims=True)
    return lax.rsqrt(ss / (x.shape[-1] if width is None else width) + EPS)


def _silu(g):
    return g / (1.0 + jnp.exp(-g))


def _dot(a, b):
    return jnp.dot(a, b, preferred_element_type=F32)


def _dot_nt(a, b):
    return lax.dot_general(a, b, (((1,), (1,)), ((), ())), preferred_element_type=F32)


def _normmm_kernel(x_ref, g_ref, w_ref, cs_ref, o_ref, h_ref):
    @pl.when(pl.program_id(1) == 0)
    def _():
        x = x_ref[...]
        h_ref[...] = (x * _rms_scale(x) * g_ref[...]).astype(BF16)

    o_ref[...] = (_dot(h_ref[...], w_ref[...]) * cs_ref[...]).astype(o_ref.dtype)


def _norm_matmul(x, g, w, colscale, out_dtype, tm, tn):
    m, k = x.shape
    n = w.shape[1]
    assert m % tm == 0 and n % tn == 0, (m, n, tm, tn)
    return pl.pallas_call(
        _normmm_kernel,
        grid=(m // tm, n // tn),
        in_specs=[
            pl.BlockSpec((tm, k), lambda i, j: (i, 0)),
            pl.BlockSpec((1, k), lambda i, j: (0, 0)),
            pl.BlockSpec((k, tn), lambda i, j: (0, j)),
            pl.BlockSpec((1, tn), lambda i, j: (0, j)),
        ],
        out_specs=pl.BlockSpec((tm, tn), lambda i, j: (i, j)),
        out_shape=jax.ShapeDtypeStruct((m, n), out_dtype),
        scratch_shapes=[pltpu.VMEM((tm, k), BF16)],
        compiler_params=pltpu.CompilerParams(
            dimension_semantics=("parallel", "arbitrary"), vmem_limit_bytes=VMEM_LIMIT),
        name="norm_matmul",
    )(x, g.reshape(1, k), w, colscale.reshape(1, n))


def _mask_top(x, keep, fill):
    n = keep.shape[0]
    top = jnp.where(keep, x[:n], fill)
    return top if x.shape[0] == n else jnp.concatenate([top, x[n:]], axis=0)


def _walk_key_blocks(block, i, nd, unroll=1):
    assert nd % unroll == 0
    for d in reversed(range(nd)):
        block(i * nd + d, d, True)

    def body(j, carry):
        for u in range(unroll):
            block(i * nd - 1 - j * unroll - u, 0, False)
        return carry

    lax.fori_loop(0, i * (nd // unroll), body, 0)


def _sb_kernel(q_ref, k_ref, v_ref, g_ref, u_ref, o_ref, acc_ref, carry_ref, *, tq):
    i = pl.program_id(2)
    nsub = tq // SB_SUB
    u = u_ref[...]
    row = lax.broadcasted_iota(jnp.int32, (SB_SUB, SB_SUB), 0)
    col = lax.broadcasted_iota(jnp.int32, (SB_SUB, SB_SUB), 1)
    causal = col < row
    acc_ref[...] = jnp.zeros_like(acc_ref)
    carry_ref[...] = jnp.zeros_like(carry_ref)

    def rows(r):
        return slice(r * SB_SUB, (r + 1) * SB_SUB)

    def per_sub_block(fn, x):
        return jnp.concatenate([fn(x[rows(r)]) for r in range(nsub)], axis=0)

    def step(o, diagonal):
        starts, zs = [], []
        for r in range(nsub):
            kb = i * nsub + r - o
            if not diagonal:
                carry_ref[rows(r), :] = jnp.where(kb >= 0, carry_ref[rows(r), :], SB_NO_KEYS_LEFT)
                kb = jnp.maximum(kb, 0)
            starts.append(pl.multiple_of(kb * SB_SUB, SB_SUB))
            zs.append(_dot_nt(q_ref[rows(r), :], k_ref[pl.ds(starts[r], SB_SUB), :]))
        z = jnp.concatenate(zs, axis=0)
        neg_log = jnp.maximum(z, jnp.log(1.0 + jnp.exp2(jnp.minimum(z, EXP2_CLAMP))) * LOG2_E)
        if diagonal:
            neg_log = per_sub_block(lambda t: jnp.where(causal, t, 0.0), neg_log)
        suffix = _dot(neg_log.astype(BF16), u)
        seen = carry_ref[...]
        a = jnp.exp2(z - suffix - jnp.concatenate([seen] * (SB_SUB // LANES), axis=1))
        if diagonal:
            a = per_sub_block(lambda t: jnp.where(causal, t, 0.0), a)
        a = a.astype(BF16)
        for r in range(nsub):
            acc_ref[rows(r), :] += _dot(a[rows(r)], v_ref[pl.ds(starts[r], SB_SUB), :])
        carry_ref[...] = seen + jnp.sum(neg_log, axis=-1, keepdims=True)

    step(0, True)
    step(1, False)

    def more(state):
        o, lightest = state
        return jnp.logical_and(o < (i + 1) * nsub, lightest < SB_DEAD_LOG2)

    def advance(state):
        step(state[0], False)
        return state[0] + 1, jnp.min(carry_ref[...])

    lax.while_loop(more, advance, (2, jnp.min(carry_ref[...])))
    o_ref[...] = (acc_ref[...] * _silu(g_ref[...].astype(F32))).astype(o_ref.dtype)


def _sb_attention(pa, batch, seq):
    tq = SB_TQ
    assert seq % tq == 0 and tq % SB_SUB == 0, (seq, tq)
    nq = seq // tq
    h0 = N_SB_HEADS
    u = jnp.tril(jnp.ones((SB_SUB, SB_SUB), F32)).astype(BF16)
    return pl.pallas_call(
        functools.partial(_sb_kernel, tq=tq),
        grid=(batch, N_SB_HEADS, nq),
        in_specs=[
            pl.BlockSpec((tq, HEAD_DIM), lambda b, h, i: (b * nq + i, h)),
            pl.BlockSpec((seq, HEAD_DIM), lambda b, h, i: (b, h0 + h)),
            pl.BlockSpec((seq, HEAD_DIM), lambda b, h, i: (b, 2 * h0 + h)),
            pl.BlockSpec((tq, HEAD_DIM), lambda b, h, i: (b * nq + i, 3 * h0 + h)),
            pl.BlockSpec((SB_SUB, SB_SUB), lambda b, h, i: (0, 0)),
        ],
        out_specs=pl.BlockSpec((tq, HEAD_DIM), lambda b, h, i: (b * nq + i, h)),
        out_shape=jax.ShapeDtypeStruct((batch * seq, SB_W), BF16),
        scratch_shapes=[pltpu.VMEM((tq, HEAD_DIM), F32), pltpu.VMEM((tq, LANES), F32)],
        compiler_params=pltpu.CompilerParams(
            dimension_semantics=("parallel", "parallel", "arbitrary"), vmem_limit_bytes=VMEM_LIMIT),
        name="sb_attention",
    )(pa, pa, pa, pa, u)


def _mla_kernel(bound_ref, q_ref, k_ref, v_ref, g_ref, o_ref, acc_ref, m_ref, den_ref, *, tq, tk):
    i = pl.program_id(2)
    row = lax.broadcasted_iota(jnp.int32, (tk, tk), 0)
    col = lax.broadcasted_iota(jnp.int32, (tk, tk), 1)
    causal = col <= row
    nlane = tk // LANES
    bound = bound_ref[0]
    acc_ref[...] = jnp.zeros_like(acc_ref)
    den_ref[...] = jnp.zeros_like(den_ref)

    def scores(kb, d, diagonal):
        start = pl.multiple_of(kb * tk, tk)
        s = _dot_nt(q_ref[d * tk:, :], k_ref[pl.ds(start, tk), :])
        return _mask_top(s, causal, -jnp.inf) if diagonal else s

    def finish(den):
        o_ref[...] = (acc_ref[...] / den * _silu(g_ref[...].astype(F32))).astype(o_ref.dtype)

    def bounded():
        def block(kb, d, diagonal):
            r0 = d * tk
            s = scores(kb, d, diagonal)
            ps = [jnp.exp2(s[:, c * LANES:(c + 1) * LANES] - bound) for c in range(nlane)]
            den_ref[r0:, :] += functools.reduce(lambda a, b: a + b, ps)
            p = jnp.concatenate(ps, axis=1).astype(BF16)
            acc_ref[r0:, :] += _dot(p, v_ref[pl.ds(pl.multiple_of(kb * tk, tk), tk), :])

        _walk_key_blocks(block, i, tq // tk, unroll=tq // tk)
        finish(jnp.sum(den_ref[...], axis=-1, keepdims=True))

    def online():
        m_ref[...] = jnp.full_like(m_ref, -jnp.inf)

        def block(kb, d, diagonal):
            r0 = d * tk
            s = scores(kb, d, diagonal)
            m_old = m_ref[r0:, :]
            m_new = jnp.maximum(m_old, jnp.max(s, axis=-1, keepdims=True))
            alpha = jnp.exp2(m_old - m_new)
            p = jnp.concatenate([jnp.exp2(s[:, c * LANES:(c + 1) * LANES] - m_new) for c in range(nlane)], axis=1)
            den_ref[r0:, :] = alpha * den_ref[r0:, :] + jnp.sum(p, axis=-1, keepdims=True)
            pv = _dot(p.astype(BF16), v_ref[pl.ds(pl.multiple_of(kb * tk, tk), tk), :])
            acc_ref[r0:, :] = alpha * acc_ref[r0:, :] + pv
            m_ref[r0:, :] = m_new

        _walk_key_blocks(block, i, tq // tk)
        finish(den_ref[...])

    lax.cond(bound <= MLA_SAFE_BOUND, bounded, online)


def _mla_score_bound(g_q_nope, g_q_rope, g_k_nope, g_k_rope):
    def norm(g_nope, g_rope):
        return jnp.sqrt(QK_NOPE_DIM * jnp.max(g_nope * g_nope) + QK_ROPE_DIM * jnp.max(g_rope * g_rope))

    return (MLA_BOUND_SLACK * MLA_SCALE * norm(g_q_nope, g_q_rope) * norm(g_k_nope, g_k_rope)).reshape(1)


def _mla_attention(bound, qx, kx, v, gq, batch, seq):
    tq, tk = MLA_TQ, MLA_TK
    assert seq % tq == 0 and tq % tk == 0 and tk % LANES == 0, (seq, tq, tk)
    nq = seq // tq
    return pl.pallas_call(
        functools.partial(_mla_kernel, tq=tq, tk=tk),
        grid=(batch, N_MLA_HEADS, nq),
        in_specs=[
            pl.BlockSpec(memory_space=pltpu.SMEM),
            pl.BlockSpec((tq, QK_PAD), lambda b, h, i: (b * nq + i, h)),
            pl.BlockSpec((seq, QK_PAD), lambda b, h, i: (b, h)),
            pl.BlockSpec((seq, V_HEAD_DIM), lambda b, h, i: (b, h)),
            pl.BlockSpec((tq, V_HEAD_DIM), lambda b, h, i: (b * nq + i, h)),
        ],
        out_specs=pl.BlockSpec((tq, V_HEAD_DIM), lambda b, h, i: (b * nq + i, h)),
        out_shape=jax.ShapeDtypeStruct((batch * seq, MLA_W), BF16),
        scratch_shapes=[pltpu.VMEM((tq, V_HEAD_DIM), F32), pltpu.VMEM((tq, LANES), F32),
                        pltpu.VMEM((tq, LANES), F32)],
        compiler_params=pltpu.CompilerParams(
            dimension_semantics=("parallel", "parallel", "arbitrary"), vmem_limit_bytes=VMEM_LIMIT),
        name="mla_attention",
    )(bound, qx, kx, v, gq)


def _out_kernel(main_ref, qm_ref, gm_ref, mkv_ref, gq_ref, gk_ref, x_ref, w_ref, o_ref, *, main_w):
    heads = []
    for h in range(N_MEM_HEADS):
        lo, hi = h * HEAD_DIM, (h + 1) * HEAD_DIM
        mk = mkv_ref[0, :, lo:hi]
        mk = (mk * _rms_scale(mk) * gk_ref[...]).astype(BF16)
        mv = mkv_ref[0, :, MEM_W + lo:MEM_W + hi].astype(BF16)
        q = qm_ref[:, lo:hi].astype(F32)
        q = (q * _rms_scale(q) * gq_ref[...]).astype(BF16)
        s = _dot_nt(q, mk) * HEAD_DIM ** -0.5
        p = jnp.exp(s - jnp.max(s, axis=-1, keepdims=True))
        mo = _dot(p.astype(BF16), mv) / jnp.sum(p, axis=-1, keepdims=True)
        heads.append((mo * _silu(gm_ref[:, lo:hi].astype(F32))).astype(BF16))
    acc = _dot(main_ref[...], w_ref[:main_w, :]) + _dot(jnp.concatenate(heads, axis=1), w_ref[main_w:, :])
    o_ref[...] = x_ref[...] + acc


def _mem_out_proj(main, side, qm_blk, gm_blk, mkv, g_q, g_k, x, w, seq):
    m, main_w = main.shape
    d = w.shape[1]
    mem_len = mkv.shape[1]
    tm = OUT_TM
    assert seq % tm == 0, (seq, tm)
    per_b = seq // tm
    return pl.pallas_call(
        functools.partial(_out_kernel, main_w=main_w),
        grid=(m // tm,),
        in_specs=[
            pl.BlockSpec((tm, main_w), lambda i: (i, 0)),
            pl.BlockSpec((tm, MEM_W), lambda i: (i, qm_blk)),
            pl.BlockSpec((tm, MEM_W), lambda i: (i, gm_blk)),
            pl.BlockSpec((1, mem_len, 2 * MEM_W), lambda i: (i // per_b, 0, 0)),
            pl.BlockSpec((1, HEAD_DIM), lambda i: (0, 0)),
            pl.BlockSpec((1, HEAD_DIM), lambda i: (0, 0)),
            pl.BlockSpec((tm, d), lambda i: (i, 0)),
            pl.BlockSpec((main_w + MEM_W, d), lambda i: (0, 0), pipeline_mode=pl.Buffered(1)),
        ],
        out_specs=pl.BlockSpec((tm, d), lambda i: (i, 0)),
        out_shape=jax.ShapeDtypeStruct((m, d), F32),
        compiler_params=pltpu.CompilerParams(
            dimension_semantics=("parallel",), vmem_limit_bytes=VMEM_LIMIT),
        name="mem_out_proj",
    )(main, side, side, mkv, g_q.reshape(1, HEAD_DIM), g_k.reshape(1, HEAD_DIM), x, w)


def _rope_tables(pos_ref, c_ref):
    ang = pos_ref[...].astype(F32) * c_ref[0:1, :]
    cosv, sinv = jnp.cos(ang), jnp.sin(ang)
    return cosv * c_ref[1:2, :], sinv * c_ref[2:3, :], sinv * c_ref[3:4, :]


def _rope(x, tables):
    c, s1, s2 = tables
    return x * c + pltpu.roll(x, LANES - QK_ROPE_DIM // 2, 1) * s1 + pltpu.roll(x, QK_ROPE_DIM // 2, 1) * s2


def _kv_kernel(x_ref, pos_ref, c_ref, gx_ref, wd_ref, gc_ref, wu_ref, gkn_ref, gkr_ref, kx_ref, v_ref):
    x = x_ref[...]
    h = (x * _rms_scale(x) * gx_ref[...]).astype(BF16)
    c = _dot(h, wd_ref[...])
    ckv = c[:, :KV_LORA_RANK]
    cn = (ckv * _rms_scale(ckv) * gc_ref[...]).astype(BF16)
    kv = _dot(cn, wu_ref[...])
    kr = c[:, KV_LORA_RANK:]
    kr = kr * _rms_scale(kr, QK_ROPE_DIM) * gkr_ref[...]
    k_rope = _rope(kr, _rope_tables(pos_ref, c_ref)).astype(BF16)
    for hd in range(N_MLA_HEADS):
        base = hd * QK_PAD
        kn = kv[:, base:base + QK_NOPE_DIM]
        kx_ref[:, base:base + QK_NOPE_DIM] = (kn * _rms_scale(kn) * gkn_ref[...]).astype(BF16)
        kx_ref[:, base + QK_NOPE_DIM:base + QK_PAD] = k_rope
        v_ref[:, hd * V_HEAD_DIM:(hd + 1) * V_HEAD_DIM] = kv[:, base + QK_NOPE_DIM:base + QK_PAD].astype(BF16)


def _q_kernel(x_ref, pos_ref, c_ref, gx_ref, win_ref, gl_ref, wuq_ref, gqn_ref, gqr_ref, qx_ref, side_ref):
    x = x_ref[...]
    h = (x * _rms_scale(x) * gx_ref[...]).astype(BF16)
    ql = _dot(h, win_ref[:, :Q_LORA_RANK])
    qn = (ql * _rms_scale(ql) * gl_ref[...]).astype(BF16)
    q = _dot(qn, wuq_ref[...])
    tables = _rope_tables(pos_ref, c_ref)
    for hd in range(N_MLA_HEADS):
        base = hd * QK_PAD
        qnope = q[:, base:base + QK_NOPE_DIM]
        qx_ref[:, base:base + QK_NOPE_DIM] = (qnope * _rms_scale(qnope) * gqn_ref[...] * MLA_SCALE).astype(BF16)
        qr = q[:, base + QK_NOPE_DIM:base + QK_PAD]
        qr = qr * _rms_scale(qr, QK_ROPE_DIM) * gqr_ref[...]
        qx_ref[:, base + QK_NOPE_DIM:base + QK_PAD] = (_rope(qr, tables) * MLA_SCALE).astype(BF16)
    side_ref[...] = _dot(h, win_ref[:, Q_LORA_RANK:]).astype(BF16)


def _row_call(body, x, pos, consts, small, outs, tm, name):
    m, d = x.shape
    assert m % tm == 0, (m, tm)
    resident = [pl.BlockSpec(a.shape, lambda i: (0, 0), pipeline_mode=pl.Buffered(1)) for a in (consts, *small)]
    return pl.pallas_call(
        body,
        grid=(m // tm,),
        in_specs=[pl.BlockSpec((tm, d), lambda i: (i, 0)), pl.BlockSpec((tm, 1), lambda i: (i, 0))] + resident,
        out_specs=[pl.BlockSpec((tm, w), lambda i: (i, 0)) for w in outs],
        out_shape=[jax.ShapeDtypeStruct((m, w), BF16) for w in outs],
        compiler_params=pltpu.CompilerParams(
            dimension_semantics=("parallel",), vmem_limit_bytes=VMEM_LIMIT),
        name=name,
    )(x, pos, consts, *small)


def _rope_consts():
    half = QK_ROPE_DIM // 2
    inv_freq = jnp.power(ROPE_THETA, -jnp.arange(0, QK_ROPE_DIM, 2, dtype=F32) / QK_ROPE_DIM)
    lane = np.arange(LANES)
    rows = jnp.zeros((8, LANES), F32)
    rows = rows.at[0, :QK_ROPE_DIM].set(jnp.concatenate([inv_freq, inv_freq]))
    rows = rows.at[1].set(jnp.asarray(lane < QK_ROPE_DIM, F32))
    rows = rows.at[2].set(jnp.asarray(-(lane < half).astype(np.float32)))
    rows = rows.at[3].set(jnp.asarray(((lane >= half) & (lane < QK_ROPE_DIM)).astype(np.float32)))
    return rows


def _pad_cols(a, width):
    return jnp.pad(a, [(0, 0)] * (a.ndim - 1) + [(0, width - a.shape[-1])])


def kernel(x, mem, positions, a_norm, a_w_in, a_w_out, kv_norm, w_dkv, g_ckv, w_ukv, g_k_nope, g_k_rope, b_norm, b_w_in, b_g_q_lat, b_w_uq, b_g_q_nope, b_g_q_rope, b_w_out, mem_norm, w_mem_kv, g_mem_q, g_mem_k):
    batch, seq, d = x.shape
    m = batch * seq
    mem_len = mem.shape[1]
    x2 = x.reshape(m, d)
    mem2 = mem.reshape(batch * mem_len, d)
    pos = positions.reshape(m, 1)
    consts = _rope_consts()
    row = lambda g: g.reshape(1, -1)

    def mem_kv(layer):
        w = w_mem_kv[layer].astype(BF16)
        mkv = _norm_matmul(mem2, mem_norm[layer], w, jnp.ones((2 * MEM_W,), F32), F32, *MEMKV_TILES)
        return mkv.reshape(batch, mem_len, 2 * MEM_W)

    a_in_w = a_w_in.shape[-1]
    qscale = jnp.ones((a_in_w,), F32).at[:SB_W].set(HEAD_DIM ** -0.5 * LOG2_E)
    pa = _norm_matmul(x2, a_norm[0], a_w_in[0].astype(BF16), qscale, BF16, INPROJ_TM, a_in_w // INPROJ_NSPLIT)
    sb = _sb_attention(pa, batch, seq)
    qm_blk = 4 * SB_W // MEM_W
    x2 = _mem_out_proj(sb, pa, qm_blk, qm_blk + 1, mem_kv(0), g_mem_q[0], g_mem_k[0],
                       x2, a_w_out[0].astype(BF16), seq)

    wd = _pad_cols(w_dkv, KV_LORA_RANK + LANES).astype(BF16)
    kx, v = _row_call(
        _kv_kernel, x2, pos, consts,
        (row(kv_norm), wd, row(g_ckv), w_ukv.astype(BF16), row(g_k_nope), _pad_cols(row(g_k_rope), LANES)),
        (N_MLA_HEADS * QK_PAD, MLA_W), KV_TM, "mla_kv_side")

    wuq = b_w_uq[0].reshape(Q_LORA_RANK, N_MLA_HEADS, QK_NOPE_DIM + QK_ROPE_DIM)
    wuq = _pad_cols(wuq, QK_PAD).reshape(Q_LORA_RANK, N_MLA_HEADS * QK_PAD).astype(BF16)
    qx, side = _row_call(
        _q_kernel, x2, pos, consts,
        (row(b_norm[0]), b_w_in[0].astype(BF16), row(b_g_q_lat[0]), wuq, row(b_g_q_nope[0]),
         _pad_cols(row(b_g_q_rope[0]), LANES)),
        (N_MLA_HEADS * QK_PAD, MLA_W + 2 * MEM_W), Q_TM, "mla_q_side")
    bound = _mla_score_bound(b_g_q_nope[0], b_g_q_rope[0], g_k_nope, g_k_rope)
    att = _mla_attention(bound, qx, kx, v, side, batch, seq)
    qm_blk = MLA_W // MEM_W
    x2 = _mem_out_proj(att, side, qm_blk, qm_blk + 1, mem_kv(1), g_mem_q[1], g_mem_k[1],
                       x2, b_w_out[0].astype(BF16), seq)
    return x2.reshape(batch, seq, d)
```

```python
import functools
import math

import jax
import jax.numpy as jnp
import numpy as np
from jax import lax
from jax.experimental import pallas as pl
from jax.experimental.pallas import tpu as pltpu

F32 = jnp.float32
BF16 = jnp.bfloat16

HEAD_DIM = 128
N_SB_HEADS = 12
N_MEM_HEADS = 4
N_MLA_HEADS = 12
Q_LORA_RANK = 512
KV_LORA_RANK = 512
QK_NOPE_DIM = 128
QK_ROPE_DIM = 64
V_HEAD_DIM = 128
ROPE_THETA = 10000.0
EPS = 1e-6
SB_W = N_SB_HEADS * HEAD_DIM
MEM_W = N_MEM_HEADS * HEAD_DIM
MLA_W = N_MLA_HEADS * V_HEAD_DIM
LOG2_E = math.log2(math.e)
LANES = 128
QK_PAD = 2 * LANES
VMEM_LIMIT = 56 * 1024 * 1024

INPROJ_TM, INPROJ_NSPLIT = 512, 2
MEMKV_TILES = (256, 512)
OUT_TM = 512
KV_TM = 1024
Q_TM = 512
SB_TQ = 4096
MLA_TQ, MLA_TK = 2048, 512
SB_SUB = 2 * LANES
EXP2_CLAMP = 126.0
SB_DEAD_LOG2 = 150.0
SB_NO_KEYS_LEFT = 1e30
MLA_SAFE_BOUND = 60.0
MLA_SCALE = (QK_NOPE_DIM + QK_ROPE_DIM) ** -0.5 * LOG2_E
MLA_BOUND_SLACK = 1.02


def _rms_scale(x, width=None):
    ss = jnp.sum(x * x, axis=-1, keepdims=True)
    return lax.rsqrt(ss / (x.shape[-1] if width is None else width) + EPS)


def _silu(g):
    return g / (1.0 + jnp.exp(-g))


def _dot(a, b):
    return jnp.dot(a, b, preferred_element_type=F32)


def _dot_nt(a, b):
    return lax.dot_general(a, b, (((1,), (1,)), ((), ())), preferred_element_type=F32)


def _normmm_kernel(x_ref, g_ref, w_ref, cs_ref, o_ref, h_ref):
    @pl.when(pl.program_id(1) == 0)
    def _():
        x = x_ref[...]
        h_ref[...] = (x * _rms_scale(x) * g_ref[...]).astype(BF16)

    o_ref[...] = (_dot(h_ref[...], w_ref[...]) * cs_ref[...]).astype(o_ref.dtype)


def _norm_matmul(x, g, w, colscale, out_dtype, tm, tn):
    m, k = x.shape
    n = w.shape[1]
    assert m % tm == 0 and n % tn == 0, (m, n, tm, tn)
    return pl.pallas_call(
        _normmm_kernel,
        grid=(m // tm, n // tn),
        in_specs=[
            pl.BlockSpec((tm, k), lambda i, j: (i, 0)),
            pl.BlockSpec((1, k), lambda i, j: (0, 0)),
            pl.BlockSpec((k, tn), lambda i, j: (0, j)),
            pl.BlockSpec((1, tn), lambda i, j: (0, j)),
        ],
        out_specs=pl.BlockSpec((tm, tn), lambda i, j: (i, j)),
        out_shape=jax.ShapeDtypeStruct((m, n), out_dtype),
        scratch_shapes=[pltpu.VMEM((tm, k), BF16)],
        compiler_params=pltpu.CompilerParams(
            dimension_semantics=("parallel", "arbitrary"), vmem_limit_bytes=VMEM_LIMIT),
        name="norm_matmul",
    )(x, g.reshape(1, k), w, colscale.reshape(1, n))


def _mask_top(x, keep, fill):
    n = keep.shape[0]
    top = jnp.where(keep, x[:n], fill)
    return top if x.shape[0] == n else jnp.concatenate([top, x[n:]], axis=0)


def _walk_key_blocks(block, i, nd):
    for d in reversed(range(nd)):
        block(i * nd + d, d, True)

    def body(j, carry):
        block(i * nd - 1 - j, 0, False)
        return carry

    lax.fori_loop(0, i * nd, body, 0)


def _sb_kernel(q_ref, k_ref, v_ref, g_ref, u_ref, o_ref, acc_ref, carry_ref, *, tq):
    i = pl.program_id(2)
    nsub = tq // SB_SUB
    u = u_ref[...]
    row = lax.broadcasted_iota(jnp.int32, (SB_SUB, SB_SUB), 0)
    col = lax.broadcasted_iota(jnp.int32, (SB_SUB, SB_SUB), 1)
    causal = col < row
    acc_ref[...] = jnp.zeros_like(acc_ref)
    carry_ref[...] = jnp.zeros_like(carry_ref)

    def rows(r):
        return slice(r * SB_SUB, (r + 1) * SB_SUB)

    def per_sub_block(fn, x):
        return jnp.concatenate([fn(x[rows(r)]) for r in range(nsub)], axis=0)

    def step(o, diagonal):
        starts, zs = [], []
        for r in range(nsub):
            kb = i * nsub + r - o
            if not diagonal:
                carry_ref[rows(r), :] = jnp.where(kb >= 0, carry_ref[rows(r), :], SB_NO_KEYS_LEFT)
                kb = jnp.maximum(kb, 0)
            starts.append(pl.multiple_of(kb * SB_SUB, SB_SUB))
            zs.append(_dot_nt(q_ref[rows(r), :], k_ref[pl.ds(starts[r], SB_SUB), :]))
        z = jnp.concatenate(zs, axis=0)
        neg_log = jnp.maximum(z, jnp.log(1.0 + jnp.exp2(jnp.minimum(z, EXP2_CLAMP))) * LOG2_E)
        if diagonal:
            neg_log = per_sub_block(lambda t: jnp.where(causal, t, 0.0), neg_log)
        suffix = _dot(neg_log.astype(BF16), u)
        seen = carry_ref[...]
        a = jnp.exp2(z - suffix - jnp.concatenate([seen] * (SB_SUB // LANES), axis=1))
        if diagonal:
            a = per_sub_block(lambda t: jnp.where(causal, t, 0.0), a)
        a = a.astype(BF16)
        for r in range(nsub):
            acc_ref[rows(r), :] += _dot(a[rows(r)], v_ref[pl.ds(starts[r], SB_SUB), :])
        carry_ref[...] = seen + jnp.sum(neg_log, axis=-1, keepdims=True)

    step(0, True)
    step(1, False)

    def more(state):
        o, lightest = state
        return jnp.logical_and(o < (i + 1) * nsub, lightest < SB_DEAD_LOG2)

    def advance(state):
        step(state[0], False)
        return state[0] + 1, jnp.min(carry_ref[...])

    lax.while_loop(more, advance, (2, jnp.min(carry_ref[...])))
    o_ref[...] = (acc_ref[...] * _silu(g_ref[...].astype(F32))).astype(o_ref.dtype)


def _sb_attention(pa, batch, seq):
    tq = SB_TQ
    assert seq % tq == 0 and tq % SB_SUB == 0, (seq, tq)
    nq = seq // tq
    h0 = N_SB_HEADS
    u = jnp.tril(jnp.ones((SB_SUB, SB_SUB), F32)).astype(BF16)
    return pl.pallas_call(
        functools.partial(_sb_kernel, tq=tq),
        grid=(batch, N_SB_HEADS, nq),
        in_specs=[
            pl.BlockSpec((tq, HEAD_DIM), lambda b, h, i: (b * nq + i, h)),
            pl.BlockSpec((seq, HEAD_DIM), lambda b, h, i: (b, h0 + h)),
            pl.BlockSpec((seq, HEAD_DIM), lambda b, h, i: (b, 2 * h0 + h)),
            pl.BlockSpec((tq, HEAD_DIM), lambda b, h, i: (b * nq + i, 3 * h0 + h)),
            pl.BlockSpec((SB_SUB, SB_SUB), lambda b, h, i: (0, 0)),
        ],
        out_specs=pl.BlockSpec((tq, HEAD_DIM), lambda b, h, i: (b * nq + i, h)),
        out_shape=jax.ShapeDtypeStruct((batch * seq, SB_W), BF16),
        scratch_shapes=[pltpu.VMEM((tq, HEAD_DIM), F32), pltpu.VMEM((tq, LANES), F32)],
        compiler_params=pltpu.CompilerParams(
            dimension_semantics=("parallel", "parallel", "arbitrary"), vmem_limit_bytes=VMEM_LIMIT),
        name="sb_attention",
    )(pa, pa, pa, pa, u)


def _mla_kernel(bound_ref, q_ref, k_ref, v_ref, g_ref, o_ref, acc_ref, m_ref, den_ref, *, tq, tk):
    i = pl.program_id(2)
    row = lax.broadcasted_iota(jnp.int32, (tk, tk), 0)
    col = lax.broadcasted_iota(jnp.int32, (tk, tk), 1)
    causal = col <= row
    nlane = tk // LANES
    nd = tq // tk
    bound = bound_ref[0]

    def scores(kb, d, diagonal):
        start = pl.multiple_of(kb * tk, tk)
        s = _dot_nt(q_ref[d * tk:, :], k_ref[pl.ds(start, tk), :])
        return _mask_top(s, causal, -jnp.inf) if diagonal else s

    def finish(den):
        o_ref[...] = (acc_ref[...] / den * _silu(g_ref[...].astype(F32))).astype(o_ref.dtype)

    def add(xs):
        return functools.reduce(lambda a, b: a + b, xs)

    def bounded():
        def tile(kb, d, diagonal):
            s = scores(kb, d, diagonal)
            ps = [jnp.exp2(s[:, c * LANES:(c + 1) * LANES] - bound) for c in range(nlane)]
            p = jnp.concatenate(ps, axis=1).astype(BF16)
            return add(ps), _dot(p, v_ref[pl.ds(pl.multiple_of(kb * tk, tk), tk), :])

        dens, pvs = zip(*[tile(i * nd + d, d, True) for d in range(nd)])
        for r in range(nd):
            rows = slice(r * tk, (r + 1) * tk)
            den_ref[rows, :] = add([dens[d][(r - d) * tk:(r - d + 1) * tk] for d in range(r + 1)])
            acc_ref[rows, :] = add([pvs[d][(r - d) * tk:(r - d + 1) * tk] for d in range(r + 1)])

        def older(j, carry):
            dens, pvs = zip(*[tile((i - j) * nd - 1 - u, 0, False) for u in range(nd)])
            den_ref[...] += add(dens)
            acc_ref[...] += add(pvs)
            return carry

        lax.fori_loop(0, i, older, 0)
        finish(jnp.sum(den_ref[...], axis=-1, keepdims=True))

    def online():
        acc_ref[...] = jnp.zeros_like(acc_ref)
        den_ref[...] = jnp.zeros_like(den_ref)
        m_ref[...] = jnp.full_like(m_ref, -jnp.inf)

        def block(kb, d, diagonal):
            r0 = d * tk
            s = scores(kb, d, diagonal)
            m_old = m_ref[r0:, :]
            m_new = jnp.maximum(m_old, jnp.max(s, axis=-1, keepdims=True))
            alpha = jnp.exp2(m_old - m_new)
            p = jnp.concatenate([jnp.exp2(s[:, c * LANES:(c + 1) * LANES] - m_new) for c in range(nlane)], axis=1)
            den_ref[r0:, :] = alpha * den_ref[r0:, :] + jnp.sum(p, axis=-1, keepdims=True)
            pv = _dot(p.astype(BF16), v_ref[pl.ds(pl.multiple_of(kb * tk, tk), tk), :])
            acc_ref[r0:, :] = alpha * acc_ref[r0:, :] + pv
            m_ref[r0:, :] = m_new

        _walk_key_blocks(block, i, nd)
        finish(den_ref[...])

    lax.cond(bound <= MLA_SAFE_BOUND, bounded, online)


def _mla_score_bound(g_q_nope, g_q_rope, g_k_nope, g_k_rope):
    def norm(g_nope, g_rope):
        return jnp.sqrt(QK_NOPE_DIM * jnp.max(g_nope * g_nope) + QK_ROPE_DIM * jnp.max(g_rope * g_rope))

    return (MLA_BOUND_SLACK * MLA_SCALE * norm(g_q_nope, g_q_rope) * norm(g_k_nope, g_k_rope)).reshape(1)


def _mla_attention(bound, qx, kx, v, gq, batch, seq):
    tq, tk = MLA_TQ, MLA_TK
    assert seq % tq == 0 and tq % tk == 0 and tk % LANES == 0, (seq, tq, tk)
    nq = seq // tq
    return pl.pallas_call(
        functools.partial(_mla_kernel, tq=tq, tk=tk),
        grid=(batch, N_MLA_HEADS, nq),
        in_specs=[
            pl.BlockSpec(memory_space=pltpu.SMEM),
            pl.BlockSpec((tq, QK_PAD), lambda b, h, i: (b * nq + i, h)),
            pl.BlockSpec((seq, QK_PAD), lambda b, h, i: (b, h)),
            pl.BlockSpec((seq, V_HEAD_DIM), lambda b, h, i: (b, h)),
            pl.BlockSpec((tq, V_HEAD_DIM), lambda b, h, i: (b * nq + i, h)),
        ],
        out_specs=pl.BlockSpec((tq, V_HEAD_DIM), lambda b, h, i: (b * nq + i, h)),
        out_shape=jax.ShapeDtypeStruct((batch * seq, MLA_W), BF16),
        scratch_shapes=[pltpu.VMEM((tq, V_HEAD_DIM), F32), pltpu.VMEM((tq, LANES), F32),
                        pltpu.VMEM((tq, LANES), F32)],
        compiler_params=pltpu.CompilerParams(
            dimension_semantics=("parallel", "parallel", "arbitrary"), vmem_limit_bytes=VMEM_LIMIT),
        name="mla_attention",
    )(bound, qx, kx, v, gq)


def _out_kernel(main_ref, qm_ref, gm_ref, mkv_ref, gq_ref, gk_ref, x_ref, w_ref, o_ref, *, main_w):
    heads = []
    for h in range(N_MEM_HEADS):
        lo, hi = h * HEAD_DIM, (h + 1) * HEAD_DIM
        mk = mkv_ref[0, :, lo:hi]
        mk = (mk * _rms_scale(mk) * gk_ref[...]).astype(BF16)
        mv = mkv_ref[0, :, MEM_W + lo:MEM_W + hi].astype(BF16)
        q = qm_ref[:, lo:hi].astype(F32)
        q = (q * _rms_scale(q) * gq_ref[...]).astype(BF16)
        s = _dot_nt(q, mk) * HEAD_DIM ** -0.5
        p = jnp.exp(s - jnp.max(s, axis=-1, keepdims=True))
        mo = _dot(p.astype(BF16), mv) / jnp.sum(p, axis=-1, keepdims=True)
        heads.append((mo * _silu(gm_ref[:, lo:hi].astype(F32))).astype(BF16))
    acc = _dot(main_ref[...], w_ref[:main_w, :]) + _dot(jnp.concatenate(heads, axis=1), w_ref[main_w:, :])
    o_ref[...] = x_ref[...] + acc


def _mem_out_proj(main, side, qm_blk, gm_blk, mkv, g_q, g_k, x, w, seq):
    m, main_w = main.shape
    d = w.shape[1]
    mem_len = mkv.shape[1]
    tm = OUT_TM
    assert seq % tm == 0, (seq, tm)
    per_b = seq // tm
    return pl.pallas_call(
        functools.partial(_out_kernel, main_w=main_w),
        grid=(m // tm,),
        in_specs=[
            pl.BlockSpec((tm, main_w), lambda i: (i, 0)),
            pl.BlockSpec((tm, MEM_W), lambda i: (i, qm_blk)),
            pl.BlockSpec((tm, MEM_W), lambda i: (i, gm_blk)),
            pl.BlockSpec((1, mem_len, 2 * MEM_W), lambda i: (i // per_b, 0, 0)),
            pl.BlockSpec((1, HEAD_DIM), lambda i: (0, 0)),
            pl.BlockSpec((1, HEAD_DIM), lambda i: (0, 0)),
            pl.BlockSpec((tm, d), lambda i: (i, 0)),
            pl.BlockSpec((main_w + MEM_W, d), lambda i: (0, 0), pipeline_mode=pl.Buffered(1)),
        ],
        out_specs=pl.BlockSpec((tm, d), lambda i: (i, 0)),
        out_shape=jax.ShapeDtypeStruct((m, d), F32),
        compiler_params=pltpu.CompilerParams(
            dimension_semantics=("parallel",), vmem_limit_bytes=VMEM_LIMIT),
        name="mem_out_proj",
    )(main, side, side, mkv, g_q.reshape(1, HEAD_DIM), g_k.reshape(1, HEAD_DIM), x, w)


def _rope_tables(pos_ref, c_ref):
    ang = pos_ref[...].astype(F32) * c_ref[0:1, :]
    cosv, sinv = jnp.cos(ang), jnp.sin(ang)
    return cosv * c_ref[1:2, :], sinv * c_ref[2:3, :], sinv * c_ref[3:4, :]


def _rope(x, tables):
    c, s1, s2 = tables
    return x * c + pltpu.roll(x, LANES - QK_ROPE_DIM // 2, 1) * s1 + pltpu.roll(x, QK_ROPE_DIM // 2, 1) * s2


def _kv_kernel(x_ref, pos_ref, c_ref, gx_ref, wd_ref, gc_ref, wu_ref, gkn_ref, gkr_ref, kx_ref, v_ref):
    x = x_ref[...]
    h = (x * _rms_scale(x) * gx_ref[...]).astype(BF16)
    c = _dot(h, wd_ref[...])
    ckv = c[:, :KV_LORA_RANK]
    cn = (ckv * _rms_scale(ckv) * gc_ref[...]).astype(BF16)
    kv = _dot(cn, wu_ref[...])
    kr = c[:, KV_LORA_RANK:]
    kr = kr * _rms_scale(kr, QK_ROPE_DIM) * gkr_ref[...]
    k_rope = _rope(kr, _rope_tables(pos_ref, c_ref)).astype(BF16)
    for hd in range(N_MLA_HEADS):
        base = hd * QK_PAD
        kn = kv[:, base:base + QK_NOPE_DIM]
        kx_ref[:, base:base + QK_NOPE_DIM] = (kn * _rms_scale(kn) * gkn_ref[...]).astype(BF16)
        kx_ref[:, base + QK_NOPE_DIM:base + QK_PAD] = k_rope
        v_ref[:, hd * V_HEAD_DIM:(hd + 1) * V_HEAD_DIM] = kv[:, base + QK_NOPE_DIM:base + QK_PAD].astype(BF16)


def _q_kernel(x_ref, pos_ref, c_ref, gx_ref, win_ref, gl_ref, wuq_ref, gqn_ref, gqr_ref, qx_ref, side_ref):
    x = x_ref[...]
    h = (x * _rms_scale(x) * gx_ref[...]).astype(BF16)
    ql = _dot(h, win_ref[:, :Q_LORA_RANK])
    qn = (ql * _rms_scale(ql) * gl_ref[...]).astype(BF16)
    q = _dot(qn, wuq_ref[...])
    tables = _rope_tables(pos_ref, c_ref)
    for hd in range(N_MLA_HEADS):
        base = hd * QK_PAD
        qnope = q[:, base:base + QK_NOPE_DIM]
        qx_ref[:, base:base + QK_NOPE_DIM] = (qnope * _rms_scale(qnope) * gqn_ref[...] * MLA_SCALE).astype(BF16)
        qr = q[:, base + QK_NOPE_DIM:base + QK_PAD]
        qr = qr * _rms_scale(qr, QK_ROPE_DIM) * gqr_ref[...]
        qx_ref[:, base + QK_NOPE_DIM:base + QK_PAD] = (_rope(qr, tables) * MLA_SCALE).astype(BF16)
    side_ref[...] = _dot(h, win_ref[:, Q_LORA_RANK:]).astype(BF16)


def _row_call(body, x, pos, consts, small, outs, tm, name):
    m, d = x.shape
    assert m % tm == 0, (m, tm)
    resident = [pl.BlockSpec(a.shape, lambda i: (0, 0), pipeline_mode=pl.Buffered(1)) for a in (consts, *small)]
    return pl.pallas_call(
        body,
        grid=(m // tm,),
        in_specs=[pl.BlockSpec((tm, d), lambda i: (i, 0)), pl.BlockSpec((tm, 1), lambda i: (i, 0))] + resident,
        out_specs=[pl.BlockSpec((tm, w), lambda i: (i, 0)) for w in outs],
        out_shape=[jax.ShapeDtypeStruct((m, w), BF16) for w in outs],
        compiler_params=pltpu.CompilerParams(
            dimension_semantics=("parallel",), vmem_limit_bytes=VMEM_LIMIT),
        name=name,
    )(x, pos, consts, *small)


def _rope_consts():
    half = QK_ROPE_DIM // 2
    inv_freq = jnp.power(ROPE_THETA, -jnp.arange(0, QK_ROPE_DIM, 2, dtype=F32) / QK_ROPE_DIM)
    lane = np.arange(LANES)
    rows = jnp.zeros((8, LANES), F32)
    rows = rows.at[0, :QK_ROPE_DIM].set(jnp.concatenate([inv_freq, inv_freq]))
    rows = rows.at[1].set(jnp.asarray(lane < QK_ROPE_DIM, F32))
    rows = rows.at[2].set(jnp.asarray(-(lane < half).astype(np.float32)))
    rows = rows.at[3].set(jnp.asarray(((lane >= half) & (lane < QK_ROPE_DIM)).astype(np.float32)))
    return rows


def _pad_cols(a, width):
    return jnp.pad(a, [(0, 0)] * (a.ndim - 1) + [(0, width - a.shape[-1])])


def kernel(x, mem, positions, a_norm, a_w_in, a_w_out, kv_norm, w_dkv, g_ckv, w_ukv, g_k_nope, g_k_rope, b_norm, b_w_in, b_g_q_lat, b_w_uq, b_g_q_nope, b_g_q_rope, b_w_out, mem_norm, w_mem_kv, g_mem_q, g_mem_k):
    batch, seq, d = x.shape
    m = batch * seq
    mem_len = mem.shape[1]
    x2 = x.reshape(m, d)
    mem2 = mem.reshape(batch * mem_len, d)
    pos = positions.reshape(m, 1)
    consts = _rope_consts()
    row = lambda g: g.reshape(1, -1)

    def mem_kv(layer):
        w = w_mem_kv[layer].astype(BF16)
        mkv = _norm_matmul(mem2, mem_norm[layer], w, jnp.ones((2 * MEM_W,), F32), F32, *MEMKV_TILES)
        return mkv.reshape(batch, mem_len, 2 * MEM_W)

    a_in_w = a_w_in.shape[-1]
    qscale = jnp.ones((a_in_w,), F32).at[:SB_W].set(HEAD_DIM ** -0.5 * LOG2_E)
    pa = _norm_matmul(x2, a_norm[0], a_w_in[0].astype(BF16), qscale, BF16, INPROJ_TM, a_in_w // INPROJ_NSPLIT)
    sb = _sb_attention(pa, batch, seq)
    qm_blk = 4 * SB_W // MEM_W
    x2 = _mem_out_proj(sb, pa, qm_blk, qm_blk + 1, mem_kv(0), g_mem_q[0], g_mem_k[0],
                       x2, a_w_out[0].astype(BF16), seq)

    wd = _pad_cols(w_dkv, KV_LORA_RANK + LANES).astype(BF16)
    kx, v = _row_call(
        _kv_kernel, x2, pos, consts,
        (row(kv_norm), wd, row(g_ckv), w_ukv.astype(BF16), row(g_k_nope), _pad_cols(row(g_k_rope), LANES)),
        (N_MLA_HEADS * QK_PAD, MLA_W), KV_TM, "mla_kv_side")

    wuq = b_w_uq[0].reshape(Q_LORA_RANK, N_MLA_HEADS, QK_NOPE_DIM + QK_ROPE_DIM)
    wuq = _pad_cols(wuq, QK_PAD).reshape(Q_LORA_RANK, N_MLA_HEADS * QK_PAD).astype(BF16)
    qx, side = _row_call(
        _q_kernel, x2, pos, consts,
        (row(b_norm[0]), b_w_in[0].astype(BF16), row(b_g_q_lat[0]), wuq, row(b_g_q_nope[0]),
         _pad_cols(row(b_g_q_rope[0]), LANES)),
        (N_MLA_HEADS * QK_PAD, MLA_W + 2 * MEM_W), Q_TM, "mla_q_side")
    bound = _mla_score_bound(b_g_q_nope[0], b_g_q_rope[0], g_k_nope, g_k_rope)
    att = _mla_attention(bound, qx, kx, v, side, batch, seq)
    qm_blk = MLA_W // MEM_W
    x2 = _mem_out_proj(att, side, qm_blk, qm_blk + 1, mem_kv(1), g_mem_q[1], g_mem_k[1],
                       x2, b_w_out[0].astype(BF16), seq)
    return x2.reshape(batch, seq, d)
```

```python
import functools
import math

import jax
import jax.numpy as jnp
import numpy as np
from jax import lax
from jax.experimental import pallas as pl
from jax.experimental.pallas import tpu as pltpu

F32 = jnp.float32
BF16 = jnp.bfloat16

HEAD_DIM = 128
N_SB_HEADS = 12
N_MEM_HEADS = 4
N_MLA_HEADS = 12
Q_LORA_RANK = 512
KV_LORA_RANK = 512
QK_NOPE_DIM = 128
QK_ROPE_DIM = 64
V_HEAD_DIM = 128
ROPE_THETA = 10000.0
EPS = 1e-6
SB_W = N_SB_HEADS * HEAD_DIM
MEM_W = N_MEM_HEADS * HEAD_DIM
MLA_W = N_MLA_HEADS * V_HEAD_DIM
LOG2_E = math.log2(math.e)
LANES = 128
QK_PAD = 2 * LANES
VMEM_LIMIT = 56 * 1024 * 1024

INPROJ_TM, INPROJ_NSPLIT = 512, 2
MEMKV_TILES = (256, 512)
OUT_TM = 512
KV_TM = 1024
Q_TM = 512
SB_TQ = 4096
MLA_TQ, MLA_TK = 2048, 512
SB_SUB = 2 * LANES
EXP2_CLAMP = 126.0
SB_DEAD_LOG2 = 150.0
SB_NO_KEYS_LEFT = 1e30
MLA_SAFE_BOUND = 60.0
MLA_SCALE = (QK_NOPE_DIM + QK_ROPE_DIM) ** -0.5 * LOG2_E
MLA_BOUND_SLACK = 1.02


def _rms_scale(x, width=None):
    ss = jnp.sum(x * x, axis=-1, keepdims=True)
    return lax.rsqrt(ss / (x.shape[-1] if width is None else width) + EPS)


def _silu(g):
    return g / (1.0 + jnp.exp(-g))


def _dot(a, b):
    return jnp.dot(a, b, preferred_element_type=F32)


def _dot_nt(a, b):
    return lax.dot_general(a, b, (((1,), (1,)), ((), ())), preferred_element_type=F32)


def _normmm_kernel(x_ref, g_ref, w_ref, cs_ref, o_ref, h_ref):
    @pl.when(pl.program_id(1) == 0)
    def _():
        x = x_ref[...]
        h_ref[...] = (x * _rms_scale(x) * g_ref[...]).astype(BF16)

    o_ref[...] = (_dot(h_ref[...], w_ref[...]) * cs_ref[...]).astype(o_ref.dtype)


def _norm_matmul(x, g, w, colscale, out_dtype, tm, tn):
    m, k = x.shape
    n = w.shape[1]
    assert m % tm == 0 and n % tn == 0, (m, n, tm, tn)
    return pl.pallas_call(
        _normmm_kernel,
        grid=(m // tm, n // tn),
        in_specs=[
            pl.BlockSpec((tm, k), lambda i, j: (i, 0)),
            pl.BlockSpec((1, k), lambda i, j: (0, 0)),
            pl.BlockSpec((k, tn), lambda i, j: (0, j)),
            pl.BlockSpec((1, tn), lambda i, j: (0, j)),
        ],
        out_specs=pl.BlockSpec((tm, tn), lambda i, j: (i, j)),
        out_shape=jax.ShapeDtypeStruct((m, n), out_dtype),
        scratch_shapes=[pltpu.VMEM((tm, k), BF16)],
        compiler_params=pltpu.CompilerParams(
            dimension_semantics=("parallel", "arbitrary"), vmem_limit_bytes=VMEM_LIMIT),
        name="norm_matmul",
    )(x, g.reshape(1, k), w, colscale.reshape(1, n))


def _mask_top(x, keep, fill):
    n = keep.shape[0]
    top = jnp.where(keep, x[:n], fill)
    return top if x.shape[0] == n else jnp.concatenate([top, x[n:]], axis=0)


def _walk_key_blocks(block, i, nd):
    for d in reversed(range(nd)):
        block(i * nd + d, d, True)

    def body(j, carry):
        block(i * nd - 1 - j, 0, False)
        return carry

    lax.fori_loop(0, i * nd, body, 0)


def _sb_kernel(q_ref, k_ref, v_ref, g_ref, u_ref, o_ref, acc_ref, carry_ref, *, tq):
    i = pl.program_id(2)
    nsub = tq // SB_SUB
    u = u_ref[...]
    row = lax.broadcasted_iota(jnp.int32, (SB_SUB, SB_SUB), 0)
    col = lax.broadcasted_iota(jnp.int32, (SB_SUB, SB_SUB), 1)
    causal = col < row
    acc_ref[...] = jnp.zeros_like(acc_ref)
    carry_ref[...] = jnp.zeros_like(carry_ref)

    def rows(r):
        return slice(r * SB_SUB, (r + 1) * SB_SUB)

    def per_sub_block(fn, x):
        return jnp.concatenate([fn(x[rows(r)]) for r in range(nsub)], axis=0)

    def step(o, diagonal):
        starts, zs = [], []
        for r in range(nsub):
            kb = i * nsub + r - o
            if not diagonal:
                carry_ref[rows(r), :] = jnp.where(kb >= 0, carry_ref[rows(r), :], SB_NO_KEYS_LEFT)
                kb = jnp.maximum(kb, 0)
            starts.append(pl.multiple_of(kb * SB_SUB, SB_SUB))
            zs.append(_dot_nt(q_ref[rows(r), :], k_ref[pl.ds(starts[r], SB_SUB), :]))
        z = jnp.concatenate(zs, axis=0)
        neg_log = jnp.maximum(z, jnp.log(1.0 + jnp.exp2(jnp.minimum(z, EXP2_CLAMP))) * LOG2_E)
        if diagonal:
            neg_log = per_sub_block(lambda t: jnp.where(causal, t, 0.0), neg_log)
        suffix = _dot(neg_log.astype(BF16), u)
        seen = carry_ref[...]
        a = jnp.exp2(z - neg_log - suffix - jnp.concatenate([seen] * (SB_SUB // LANES), axis=1))
        if diagonal:
            a = per_sub_block(lambda t: jnp.where(causal, t, 0.0), a)
        a = a.astype(BF16)
        for r in range(nsub):
            acc_ref[rows(r), :] += _dot(a[rows(r)], v_ref[pl.ds(starts[r], SB_SUB), :])
        carry_ref[...] = seen + jnp.sum(neg_log, axis=-1, keepdims=True)

    step(0, True)
    step(1, False)

    def more(state):
        o, lightest = state
        return jnp.logical_and(o < (i + 1) * nsub, lightest < SB_DEAD_LOG2)

    def advance(state):
        step(state[0], False)
        return state[0] + 1, jnp.min(carry_ref[...])

    lax.while_loop(more, advance, (2, jnp.min(carry_ref[...])))
    o_ref[...] = (acc_ref[...] * _silu(g_ref[...].astype(F32))).astype(o_ref.dtype)


def _sb_attention(pa, batch, seq):
    tq = SB_TQ
    assert seq % tq == 0 and tq % SB_SUB == 0, (seq, tq)
    nq = seq // tq
    h0 = N_SB_HEADS
    u = jnp.tril(jnp.ones((SB_SUB, SB_SUB), F32), -1).astype(BF16)
    return pl.pallas_call(
        functools.partial(_sb_kernel, tq=tq),
        grid=(batch, N_SB_HEADS, nq),
        in_specs=[
            pl.BlockSpec((tq, HEAD_DIM), lambda b, h, i: (b * nq + i, h)),
            pl.BlockSpec((seq, HEAD_DIM), lambda b, h, i: (b, h0 + h)),
            pl.BlockSpec((seq, HEAD_DIM), lambda b, h, i: (b, 2 * h0 + h)),
            pl.BlockSpec((tq, HEAD_DIM), lambda b, h, i: (b * nq + i, 3 * h0 + h)),
            pl.BlockSpec((SB_SUB, SB_SUB), lambda b, h, i: (0, 0)),
        ],
        out_specs=pl.BlockSpec((tq, HEAD_DIM), lambda b, h, i: (b * nq + i, h)),
        out_shape=jax.ShapeDtypeStruct((batch * seq, SB_W), BF16),
        scratch_shapes=[pltpu.VMEM((tq, HEAD_DIM), F32), pltpu.VMEM((tq, LANES), F32)],
        compiler_params=pltpu.CompilerParams(
            dimension_semantics=("parallel", "parallel", "arbitrary"), vmem_limit_bytes=VMEM_LIMIT),
        name="sb_attention",
    )(pa, pa, pa, pa, u)


def _mla_kernel(bound_ref, q_ref, k_ref, v_ref, g_ref, o_ref, acc_ref, m_ref, den_ref, *, tq, tk):
    i = pl.program_id(2)
    row = lax.broadcasted_iota(jnp.int32, (tk, tk), 0)
    col = lax.broadcasted_iota(jnp.int32, (tk, tk), 1)
    causal = col <= row
    nlane = tk // LANES
    nd = tq // tk
    bound = bound_ref[0]

    def scores(kb, d, diagonal):
        start = pl.multiple_of(kb * tk, tk)
        s = _dot_nt(q_ref[d * tk:, :], k_ref[pl.ds(start, tk), :])
        return _mask_top(s, causal, -jnp.inf) if diagonal else s

    def finish(den):
        o_ref[...] = (acc_ref[...] / den * _silu(g_ref[...].astype(F32))).astype(o_ref.dtype)

    def add(xs):
        return functools.reduce(lambda a, b: a + b, xs)

    def bounded():
        def tile(kb, d, diagonal):
            s = scores(kb, d, diagonal)
            ps = [jnp.exp2(s[:, c * LANES:(c + 1) * LANES] - bound) for c in range(nlane)]
            p = jnp.concatenate(ps, axis=1).astype(BF16)
            return add(ps), _dot(p, v_ref[pl.ds(pl.multiple_of(kb * tk, tk), tk), :])

        dens, pvs = zip(*[tile(i * nd + d, d, True) for d in range(nd)])
        for r in range(nd):
            rows = slice(r * tk, (r + 1) * tk)
            den_ref[rows, :] = add([dens[d][(r - d) * tk:(r - d + 1) * tk] for d in range(r + 1)])
            acc_ref[rows, :] = add([pvs[d][(r - d) * tk:(r - d + 1) * tk] for d in range(r + 1)])

        def older(j, carry):
            dens, pvs = zip(*[tile((i - j) * nd - 1 - u, 0, False) for u in range(nd)])
            den_ref[...] += add(dens)
            acc_ref[...] += add(pvs)
            return carry

        lax.fori_loop(0, i, older, 0)
        finish(jnp.sum(den_ref[...], axis=-1, keepdims=True))

    def online():
        acc_ref[...] = jnp.zeros_like(acc_ref)
        den_ref[...] = jnp.zeros_like(den_ref)
        m_ref[...] = jnp.full_like(m_ref, -jnp.inf)

        def block(kb, d, diagonal):
            r0 = d * tk
            s = scores(kb, d, diagonal)
            m_old = m_ref[r0:, :]
            m_new = jnp.maximum(m_old, jnp.max(s, axis=-1, keepdims=True))
            alpha = jnp.exp2(m_old - m_new)
            p = jnp.concatenate([jnp.exp2(s[:, c * LANES:(c + 1) * LANES] - m_new) for c in range(nlane)], axis=1)
            den_ref[r0:, :] = alpha * den_ref[r0:, :] + jnp.sum(p, axis=-1, keepdims=True)
            pv = _dot(p.astype(BF16), v_ref[pl.ds(pl.multiple_of(kb * tk, tk), tk), :])
            acc_ref[r0:, :] = alpha * acc_ref[r0:, :] + pv
            m_ref[r0:, :] = m_new

        _walk_key_blocks(block, i, nd)
        finish(den_ref[...])

    lax.cond(bound <= MLA_SAFE_BOUND, bounded, online)


def _mla_score_bound(g_q_nope, g_q_rope, g_k_nope, g_k_rope):
    def norm(g_nope, g_rope):
        return jnp.sqrt(QK_NOPE_DIM * jnp.max(g_nope * g_nope) + QK_ROPE_DIM * jnp.max(g_rope * g_rope))

    return (MLA_BOUND_SLACK * MLA_SCALE * norm(g_q_nope, g_q_rope) * norm(g_k_nope, g_k_rope)).reshape(1)


def _mla_attention(bound, qx, kx, v, gq, batch, seq):
    tq, tk = MLA_TQ, MLA_TK
    assert seq % tq == 0 and tq % tk == 0 and tk % LANES == 0, (seq, tq, tk)
    nq = seq // tq
    return pl.pallas_call(
        functools.partial(_mla_kernel, tq=tq, tk=tk),
        grid=(batch, N_MLA_HEADS, nq),
        in_specs=[
            pl.BlockSpec(memory_space=pltpu.SMEM),
            pl.BlockSpec((tq, QK_PAD), lambda b, h, i: (b * nq + i, h)),
            pl.BlockSpec((seq, QK_PAD), lambda b, h, i: (b, h)),
            pl.BlockSpec((seq, V_HEAD_DIM), lambda b, h, i: (b, h)),
            pl.BlockSpec((tq, V_HEAD_DIM), lambda b, h, i: (b * nq + i, h)),
        ],
        out_specs=pl.BlockSpec((tq, V_HEAD_DIM), lambda b, h, i: (b * nq + i, h)),
        out_shape=jax.ShapeDtypeStruct((batch * seq, MLA_W), BF16),
        scratch_shapes=[pltpu.VMEM((tq, V_HEAD_DIM), F32), pltpu.VMEM((tq, LANES), F32),
                        pltpu.VMEM((tq, LANES), F32)],
        compiler_params=pltpu.CompilerParams(
            dimension_semantics=("parallel", "parallel", "arbitrary"), vmem_limit_bytes=VMEM_LIMIT),
        name="mla_attention",
    )(bound, qx, kx, v, gq)


def _out_kernel(main_ref, qm_ref, gm_ref, mkv_ref, gq_ref, gk_ref, x_ref, w_ref, o_ref, *, main_w):
    heads = []
    for h in range(N_MEM_HEADS):
        lo, hi = h * HEAD_DIM, (h + 1) * HEAD_DIM
        mk = mkv_ref[0, :, lo:hi]
        mk = (mk * _rms_scale(mk) * gk_ref[...]).astype(BF16)
        mv = mkv_ref[0, :, MEM_W + lo:MEM_W + hi].astype(BF16)
        q = qm_ref[:, lo:hi].astype(F32)
        q = (q * _rms_scale(q) * gq_ref[...]).astype(BF16)
        s = _dot_nt(q, mk) * HEAD_DIM ** -0.5
        p = jnp.exp(s - jnp.max(s, axis=-1, keepdims=True))
        mo = _dot(p.astype(BF16), mv) / jnp.sum(p, axis=-1, keepdims=True)
        heads.append((mo * _silu(gm_ref[:, lo:hi].astype(F32))).astype(BF16))
    acc = _dot(main_ref[...], w_ref[:main_w, :]) + _dot(jnp.concatenate(heads, axis=1), w_ref[main_w:, :])
    o_ref[...] = x_ref[...] + acc


def _mem_out_proj(main, side, qm_blk, gm_blk, mkv, g_q, g_k, x, w, seq):
    m, main_w = main.shape
    d = w.shape[1]
    mem_len = mkv.shape[1]
    tm = OUT_TM
    assert seq % tm == 0, (seq, tm)
    per_b = seq // tm
    return pl.pallas_call(
        functools.partial(_out_kernel, main_w=main_w),
        grid=(m // tm,),
        in_specs=[
            pl.BlockSpec((tm, main_w), lambda i: (i, 0)),
            pl.BlockSpec((tm, MEM_W), lambda i: (i, qm_blk)),
            pl.BlockSpec((tm, MEM_W), lambda i: (i, gm_blk)),
            pl.BlockSpec((1, mem_len, 2 * MEM_W), lambda i: (i // per_b, 0, 0)),
            pl.BlockSpec((1, HEAD_DIM), lambda i: (0, 0)),
            pl.BlockSpec((1, HEAD_DIM), lambda i: (0, 0)),
            pl.BlockSpec((tm, d), lambda i: (i, 0)),
            pl.BlockSpec((main_w + MEM_W, d), lambda i: (0, 0), pipeline_mode=pl.Buffered(1)),
        ],
        out_specs=pl.BlockSpec((tm, d), lambda i: (i, 0)),
        out_shape=jax.ShapeDtypeStruct((m, d), F32),
        compiler_params=pltpu.CompilerParams(
            dimension_semantics=("parallel",), vmem_limit_bytes=VMEM_LIMIT),
        name="mem_out_proj",
    )(main, side, side, mkv, g_q.reshape(1, HEAD_DIM), g_k.reshape(1, HEAD_DIM), x, w)


def _rope_tables(pos_ref, c_ref):
    ang = pos_ref[...].astype(F32) * c_ref[0:1, :]
    cosv, sinv = jnp.cos(ang), jnp.sin(ang)
    return cosv * c_ref[1:2, :], sinv * c_ref[2:3, :], sinv * c_ref[3:4, :]


def _rope(x, tables):
    c, s1, s2 = tables
    return x * c + pltpu.roll(x, LANES - QK_ROPE_DIM // 2, 1) * s1 + pltpu.roll(x, QK_ROPE_DIM // 2, 1) * s2


def _kv_kernel(x_ref, pos_ref, c_ref, gx_ref, wd_ref, gc_ref, wu_ref, gkn_ref, gkr_ref, kx_ref, v_ref):
    x = x_ref[...]
    h = (x * _rms_scale(x) * gx_ref[...]).astype(BF16)
    c = _dot(h, wd_ref[...])
    ckv = c[:, :KV_LORA_RANK]
    cn = (ckv * _rms_scale(ckv) * gc_ref[...]).astype(BF16)
    kv = _dot(cn, wu_ref[...])
    kr = c[:, KV_LORA_RANK:]
    kr = kr * _rms_scale(kr, QK_ROPE_DIM) * gkr_ref[...]
    k_rope = _rope(kr, _rope_tables(pos_ref, c_ref)).astype(BF16)
    for hd in range(N_MLA_HEADS):
        base = hd * QK_PAD
        kn = kv[:, base:base + QK_NOPE_DIM]
        kx_ref[:, base:base + QK_NOPE_DIM] = (kn * _rms_scale(kn) * gkn_ref[...]).astype(BF16)
        kx_ref[:, base + QK_NOPE_DIM:base + QK_PAD] = k_rope
        v_ref[:, hd * V_HEAD_DIM:(hd + 1) * V_HEAD_DIM] = kv[:, base + QK_NOPE_DIM:base + QK_PAD].astype(BF16)


def _q_kernel(x_ref, pos_ref, c_ref, gx_ref, win_ref, gl_ref, wuq_ref, gqn_ref, gqr_ref, qx_ref, side_ref):
    x = x_ref[...]
    h = (x * _rms_scale(x) * gx_ref[...]).astype(BF16)
    ql = _dot(h, win_ref[:, :Q_LORA_RANK])
    qn = (ql * _rms_scale(ql) * gl_ref[...]).astype(BF16)
    q = _dot(qn, wuq_ref[...])
    tables = _rope_tables(pos_ref, c_ref)
    for hd in range(N_MLA_HEADS):
        base = hd * QK_PAD
        qnope = q[:, base:base + QK_NOPE_DIM]
        qx_ref[:, base:base + QK_NOPE_DIM] = (qnope * _rms_scale(qnope) * gqn_ref[...] * MLA_SCALE).astype(BF16)
        qr = q[:, base + QK_NOPE_DIM:base + QK_PAD]
        qr = qr * _rms_scale(qr, QK_ROPE_DIM) * gqr_ref[...]
        qx_ref[:, base + QK_NOPE_DIM:base + QK_PAD] = (_rope(qr, tables) * MLA_SCALE).astype(BF16)
    side_ref[...] = _dot(h, win_ref[:, Q_LORA_RANK:]).astype(BF16)


def _row_call(body, x, pos, consts, small, outs, tm, name):
    m, d = x.shape
    assert m % tm == 0, (m, tm)
    resident = [pl.BlockSpec(a.shape, lambda i: (0, 0), pipeline_mode=pl.Buffered(1)) for a in (consts, *small)]
    return pl.pallas_call(
        body,
        grid=(m // tm,),
        in_specs=[pl.BlockSpec((tm, d), lambda i: (i, 0)), pl.BlockSpec((tm, 1), lambda i: (i, 0))] + resident,
        out_specs=[pl.BlockSpec((tm, w), lambda i: (i, 0)) for w in outs],
        out_shape=[jax.ShapeDtypeStruct((m, w), BF16) for w in outs],
        compiler_params=pltpu.CompilerParams(
            dimension_semantics=("parallel",), vmem_limit_bytes=VMEM_LIMIT),
        name=name,
    )(x, pos, consts, *small)


def _rope_consts():
    half = QK_ROPE_DIM // 2
    inv_freq = jnp.power(ROPE_THETA, -jnp.arange(0, QK_ROPE_DIM, 2, dtype=F32) / QK_ROPE_DIM)
    lane = np.arange(LANES)
    rows = jnp.zeros((8, LANES), F32)
    rows = rows.at[0, :QK_ROPE_DIM].set(jnp.concatenate([inv_freq, inv_freq]))
    rows = rows.at[1].set(jnp.asarray(lane < QK_ROPE_DIM, F32))
    rows = rows.at[2].set(jnp.asarray(-(lane < half).astype(np.float32)))
    rows = rows.at[3].set(jnp.asarray(((lane >= half) & (lane < QK_ROPE_DIM)).astype(np.float32)))
    return rows


def _pad_cols(a, width):
    return jnp.pad(a, [(0, 0)] * (a.ndim - 1) + [(0, width - a.shape[-1])])


def kernel(x, mem, positions, a_norm, a_w_in, a_w_out, kv_norm, w_dkv, g_ckv, w_ukv, g_k_nope, g_k_rope, b_norm, b_w_in, b_g_q_lat, b_w_uq, b_g_q_nope, b_g_q_rope, b_w_out, mem_norm, w_mem_kv, g_mem_q, g_mem_k):
    batch, seq, d = x.shape
    m = batch * seq
    mem_len = mem.shape[1]
    x2 = x.reshape(m, d)
    mem2 = mem.reshape(batch * mem_len, d)
    pos = positions.reshape(m, 1)
    consts = _rope_consts()
    row = lambda g: g.reshape(1, -1)

    def mem_kv(layer):
        w = w_mem_kv[layer].astype(BF16)
        mkv = _norm_matmul(mem2, mem_norm[layer], w, jnp.ones((2 * MEM_W,), F32), F32, *MEMKV_TILES)
        return mkv.reshape(batch, mem_len, 2 * MEM_W)

    a_in_w = a_w_in.shape[-1]
    qscale = jnp.ones((a_in_w,), F32).at[:SB_W].set(HEAD_DIM ** -0.5 * LOG2_E)
    pa = _norm_matmul(x2, a_norm[0], a_w_in[0].astype(BF16), qscale, BF16, INPROJ_TM, a_in_w // INPROJ_NSPLIT)
    sb = _sb_attention(pa, batch, seq)
    qm_blk = 4 * SB_W // MEM_W
    x2 = _mem_out_proj(sb, pa, qm_blk, qm_blk + 1, mem_kv(0), g_mem_q[0], g_mem_k[0],
                       x2, a_w_out[0].astype(BF16), seq)

    wd = _pad_cols(w_dkv, KV_LORA_RANK + LANES).astype(BF16)
    kx, v = _row_call(
        _kv_kernel, x2, pos, consts,
        (row(kv_norm), wd, row(g_ckv), w_ukv.astype(BF16), row(g_k_nope), _pad_cols(row(g_k_rope), LANES)),
        (N_MLA_HEADS * QK_PAD, MLA_W), KV_TM, "mla_kv_side")

    wuq = b_w_uq[0].reshape(Q_LORA_RANK, N_MLA_HEADS, QK_NOPE_DIM + QK_ROPE_DIM)
    wuq = _pad_cols(wuq, QK_PAD).reshape(Q_LORA_RANK, N_MLA_HEADS * QK_PAD).astype(BF16)
    qx, side = _row_call(
        _q_kernel, x2, pos, consts,
        (row(b_norm[0]), b_w_in[0].astype(BF16), row(b_g_q_lat[0]), wuq, row(b_g_q_nope[0]),
         _pad_cols(row(b_g_q_rope[0]), LANES)),
        (N_MLA_HEADS * QK_PAD, MLA_W + 2 * MEM_W), Q_TM, "mla_q_side")
    bound = _mla_score_bound(b_g_q_nope[0], b_g_q_rope[0], g_k_nope, g_k_rope)
    att = _mla_attention(bound, qx, kx, v, side, batch, seq)
    qm_blk = MLA_W // MEM_W
    x2 = _mem_out_proj(att, side, qm_blk, qm_blk + 1, mem_kv(1), g_mem_q[1], g_mem_k[1],
                       x2, b_w_out[0].astype(BF16), seq)
    return x2.reshape(batch, seq, d)
```

```python
import functools
import math

import jax
import jax.numpy as jnp
import numpy as np
from jax import lax
from jax.experimental import pallas as pl
from jax.experimental.pallas import tpu as pltpu

F32 = jnp.float32
BF16 = jnp.bfloat16

HEAD_DIM = 128
N_SB_HEADS = 12
N_MEM_HEADS = 4
N_MLA_HEADS = 12
Q_LORA_RANK = 512
KV_LORA_RANK = 512
QK_NOPE_DIM = 128
QK_ROPE_DIM = 64
V_HEAD_DIM = 128
ROPE_THETA = 10000.0
EPS = 1e-6
SB_W = N_SB_HEADS * HEAD_DIM
MEM_W = N_MEM_HEADS * HEAD_DIM
MLA_W = N_MLA_HEADS * V_HEAD_DIM
LOG2_E = math.log2(math.e)
LANES = 128
QK_PAD = 2 * LANES
VMEM_LIMIT = 56 * 1024 * 1024

INPROJ_TM, INPROJ_NSPLIT = 512, 2
MEMKV_TILES = (256, 512)
OUT_TM = 512
KV_TM = 1024
Q_TM = 512
SB_TQ = 4096
MLA_TQ, MLA_TK = 2048, 512
SB_SUB = 2 * LANES
EXP2_CLAMP = 126.0
SB_DEAD_LOG2 = 150.0
SB_NO_KEYS_LEFT = 1e30
SB_CAST_SLABS = 32
BF16_ROWS = 16
MLA_SAFE_BOUND = 60.0
MLA_SCALE = (QK_NOPE_DIM + QK_ROPE_DIM) ** -0.5 * LOG2_E
MLA_BOUND_SLACK = 1.02


def _rms_scale(x, width=None):
    ss = jnp.sum(x * x, axis=-1, keepdims=True)
    return lax.rsqrt(ss / (x.shape[-1] if width is None else width) + EPS)


def _silu(g):
    return g / (1.0 + jnp.exp(-g))


def _dot(a, b):
    return jnp.dot(a, b, preferred_element_type=F32)


def _dot_nt(a, b):
    return lax.dot_general(a, b, (((1,), (1,)), ((), ())), preferred_element_type=F32)


def _normmm_kernel(x_ref, g_ref, w_ref, cs_ref, o_ref, h_ref):
    @pl.when(pl.program_id(1) == 0)
    def _():
        x = x_ref[...]
        h_ref[...] = (x * _rms_scale(x) * g_ref[...]).astype(BF16)

    o_ref[...] = (_dot(h_ref[...], w_ref[...]) * cs_ref[...]).astype(o_ref.dtype)


def _norm_matmul(x, g, w, colscale, out_dtype, tm, tn):
    m, k = x.shape
    n = w.shape[1]
    assert m % tm == 0 and n % tn == 0, (m, n, tm, tn)
    return pl.pallas_call(
        _normmm_kernel,
        grid=(m // tm, n // tn),
        in_specs=[
            pl.BlockSpec((tm, k), lambda i, j: (i, 0)),
            pl.BlockSpec((1, k), lambda i, j: (0, 0)),
            pl.BlockSpec((k, tn), lambda i, j: (0, j)),
            pl.BlockSpec((1, tn), lambda i, j: (0, j)),
        ],
        out_specs=pl.BlockSpec((tm, tn), lambda i, j: (i, j)),
        out_shape=jax.ShapeDtypeStruct((m, n), out_dtype),
        scratch_shapes=[pltpu.VMEM((tm, k), BF16)],
        compiler_params=pltpu.CompilerParams(
            dimension_semantics=("parallel", "arbitrary"), vmem_limit_bytes=VMEM_LIMIT),
        name="norm_matmul",
    )(x, g.reshape(1, k), w, colscale.reshape(1, n))


def _mask_top(x, keep, fill):
    n = keep.shape[0]
    top = jnp.where(keep, x[:n], fill)
    return top if x.shape[0] == n else jnp.concatenate([top, x[n:]], axis=0)


def _walk_key_blocks(block, i, nd):
    for d in reversed(range(nd)):
        block(i * nd + d, d, True)

    def body(j, carry):
        block(i * nd - 1 - j, 0, False)
        return carry

    lax.fori_loop(0, i * nd, body, 0)


def _sb_kernel(*refs, tq, n_cast):
    q_ref, k_ref, v_ref, g_ref, u_ref = refs[:5]
    o_ref = refs[5 + n_cast]
    acc_ref, carry_ref = refs[6 + 2 * n_cast:]
    for src_ref, dst_ref in zip(refs[5:5 + n_cast], refs[6 + n_cast:6 + 2 * n_cast]):
        dst_ref[...] = src_ref[...].astype(BF16)

    i = pl.program_id(2)
    nsub = tq // SB_SUB
    u = u_ref[...]
    row = lax.broadcasted_iota(jnp.int32, (SB_SUB, SB_SUB), 0)
    col = lax.broadcasted_iota(jnp.int32, (SB_SUB, SB_SUB), 1)
    causal = col < row
    acc_ref[...] = jnp.zeros_like(acc_ref)
    carry_ref[...] = jnp.zeros_like(carry_ref)

    def rows(r):
        return slice(r * SB_SUB, (r + 1) * SB_SUB)

    def per_sub_block(fn, x):
        return jnp.concatenate([fn(x[rows(r)]) for r in range(nsub)], axis=0)

    def step(o, diagonal):
        starts, zs = [], []
        for r in range(nsub):
            kb = i * nsub + r - o
            if not diagonal:
                carry_ref[rows(r), :] = jnp.where(kb >= 0, carry_ref[rows(r), :], SB_NO_KEYS_LEFT)
                kb = jnp.maximum(kb, 0)
            starts.append(pl.multiple_of(kb * SB_SUB, SB_SUB))
            zs.append(_dot_nt(q_ref[rows(r), :], k_ref[pl.ds(starts[r], SB_SUB), :]))
        z = jnp.concatenate(zs, axis=0)
        neg_log = jnp.maximum(z, jnp.log(1.0 + jnp.exp2(jnp.minimum(z, EXP2_CLAMP))) * LOG2_E)
        if diagonal:
            neg_log = per_sub_block(lambda t: jnp.where(causal, t, 0.0), neg_log)
        suffix = _dot(neg_log.astype(BF16), u)
        seen = carry_ref[...]
        a = jnp.exp2(z - neg_log - suffix - jnp.concatenate([seen] * (SB_SUB // LANES), axis=1))
        if diagonal:
            a = per_sub_block(lambda t: jnp.where(causal, t, 0.0), a)
        a = a.astype(BF16)
        for r in range(nsub):
            acc_ref[rows(r), :] += _dot(a[rows(r)], v_ref[pl.ds(starts[r], SB_SUB), :])
        carry_ref[...] = seen + jnp.sum(neg_log, axis=-1, keepdims=True)

    step(0, True)
    step(1, False)

    def more(state):
        o, lightest = state
        return jnp.logical_and(o < (i + 1) * nsub, lightest < SB_DEAD_LOG2)

    def advance(state):
        step(state[0], False)
        return state[0] + 1, jnp.min(carry_ref[...])

    lax.while_loop(more, advance, (2, jnp.min(carry_ref[...])))
    o_ref[...] = (acc_ref[...] * _silu(g_ref[...].astype(F32))).astype(o_ref.dtype)


def _sb_attention(pa, batch, seq, to_bf16):
    tq = SB_TQ
    assert seq % tq == 0 and tq % SB_SUB == 0, (seq, tq)
    nq = seq // tq
    h0 = N_SB_HEADS
    u = jnp.tril(jnp.ones((SB_SUB, SB_SUB), F32), -1).astype(BF16)
    assert batch * N_SB_HEADS * nq >= SB_CAST_SLABS

    def slab(w):
        assert w.shape[0] % (SB_CAST_SLABS * BF16_ROWS) == 0, w.shape
        step = lambda b, h, i: (jnp.minimum((b * N_SB_HEADS + h) * nq + i, SB_CAST_SLABS - 1), 0)
        return pl.BlockSpec((w.shape[0] // SB_CAST_SLABS, w.shape[1]), step)

    out, *cast = pl.pallas_call(
        functools.partial(_sb_kernel, tq=tq, n_cast=len(to_bf16)),
        grid=(batch, N_SB_HEADS, nq),
        in_specs=[
            pl.BlockSpec((tq, HEAD_DIM), lambda b, h, i: (b * nq + i, h)),
            pl.BlockSpec((seq, HEAD_DIM), lambda b, h, i: (b, h0 + h)),
            pl.BlockSpec((seq, HEAD_DIM), lambda b, h, i: (b, 2 * h0 + h)),
            pl.BlockSpec((tq, HEAD_DIM), lambda b, h, i: (b * nq + i, 3 * h0 + h)),
            pl.BlockSpec((SB_SUB, SB_SUB), lambda b, h, i: (0, 0)),
        ] + [slab(w) for w in to_bf16],
        out_specs=[pl.BlockSpec((tq, HEAD_DIM), lambda b, h, i: (b * nq + i, h))] + [slab(w) for w in to_bf16],
        out_shape=[jax.ShapeDtypeStruct((batch * seq, SB_W), BF16)]
        + [jax.ShapeDtypeStruct(w.shape, BF16) for w in to_bf16],
        scratch_shapes=[pltpu.VMEM((tq, HEAD_DIM), F32), pltpu.VMEM((tq, LANES), F32)],
        compiler_params=pltpu.CompilerParams(
            dimension_semantics=("arbitrary", "arbitrary", "arbitrary"), vmem_limit_bytes=VMEM_LIMIT),
        name="sb_attention",
    )(pa, pa, pa, pa, u, *to_bf16)
    return out, cast


def _mla_kernel(bound_ref, q_ref, k_ref, v_ref, g_ref, o_ref, acc_ref, m_ref, den_ref, *, tq, tk):
    i = pl.program_id(2)
    row = lax.broadcasted_iota(jnp.int32, (tk, tk), 0)
    col = lax.broadcasted_iota(jnp.int32, (tk, tk), 1)
    causal = col <= row
    nlane = tk // LANES
    nd = tq // tk
    bound = bound_ref[0]

    def scores(kb, d, diagonal):
        start = pl.multiple_of(kb * tk, tk)
        s = _dot_nt(q_ref[d * tk:, :], k_ref[pl.ds(start, tk), :])
        return _mask_top(s, causal, -jnp.inf) if diagonal else s

    def finish(den):
        o_ref[...] = (acc_ref[...] / den * _silu(g_ref[...].astype(F32))).astype(o_ref.dtype)

    def add(xs):
        return functools.reduce(lambda a, b: a + b, xs)

    def bounded():
        def tile(kb, d, diagonal):
            s = scores(kb, d, diagonal)
            ps = [jnp.exp2(s[:, c * LANES:(c + 1) * LANES] - bound) for c in range(nlane)]
            p = jnp.concatenate(ps, axis=1).astype(BF16)
            return add(ps), _dot(p, v_ref[pl.ds(pl.multiple_of(kb * tk, tk), tk), :])

        dens, pvs = zip(*[tile(i * nd + d, d, True) for d in range(nd)])
        for r in range(nd):
            rows = slice(r * tk, (r + 1) * tk)
            den_ref[rows, :] = add([dens[d][(r - d) * tk:(r - d + 1) * tk] for d in range(r + 1)])
            acc_ref[rows, :] = add([pvs[d][(r - d) * tk:(r - d + 1) * tk] for d in range(r + 1)])

        def older(j, carry):
            dens, pvs = zip(*[tile((i - j) * nd - 1 - u, 0, False) for u in range(nd)])
            den_ref[...] += add(dens)
            acc_ref[...] += add(pvs)
            return carry

        lax.fori_loop(0, i, older, 0)
        finish(jnp.sum(den_ref[...], axis=-1, keepdims=True))

    def online():
        acc_ref[...] = jnp.zeros_like(acc_ref)
        den_ref[...] = jnp.zeros_like(den_ref)
        m_ref[...] = jnp.full_like(m_ref, -jnp.inf)

        def block(kb, d, diagonal):
            r0 = d * tk
            s = scores(kb, d, diagonal)
            m_old = m_ref[r0:, :]
            m_new = jnp.maximum(m_old, jnp.max(s, axis=-1, keepdims=True))
            alpha = jnp.exp2(m_old - m_new)
            p = jnp.concatenate([jnp.exp2(s[:, c * LANES:(c + 1) * LANES] - m_new) for c in range(nlane)], axis=1)
            den_ref[r0:, :] = alpha * den_ref[r0:, :] + jnp.sum(p, axis=-1, keepdims=True)
            pv = _dot(p.astype(BF16), v_ref[pl.ds(pl.multiple_of(kb * tk, tk), tk), :])
            acc_ref[r0:, :] = alpha * acc_ref[r0:, :] + pv
            m_ref[r0:, :] = m_new

        _walk_key_blocks(block, i, nd)
        finish(den_ref[...])

    lax.cond(bound <= MLA_SAFE_BOUND, bounded, online)


def _mla_score_bound(g_q_nope, g_q_rope, g_k_nope, g_k_rope):
    def norm(g_nope, g_rope):
        return jnp.sqrt(QK_NOPE_DIM * jnp.max(g_nope * g_nope) + QK_ROPE_DIM * jnp.max(g_rope * g_rope))

    return (MLA_BOUND_SLACK * MLA_SCALE * norm(g_q_nope, g_q_rope) * norm(g_k_nope, g_k_rope)).reshape(1)


def _mla_attention(bound, qx, kx, v, gq, batch, seq):
    tq, tk = MLA_TQ, MLA_TK
    assert seq % tq == 0 and tq % tk == 0 and tk % LANES == 0, (seq, tq, tk)
    nq = seq // tq
    return pl.pallas_call(
        functools.partial(_mla_kernel, tq=tq, tk=tk),
        grid=(batch, N_MLA_HEADS, nq),
        in_specs=[
            pl.BlockSpec(memory_space=pltpu.SMEM),
            pl.BlockSpec((tq, QK_PAD), lambda b, h, i: (b * nq + i, h)),
            pl.BlockSpec((seq, QK_PAD), lambda b, h, i: (b, h)),
            pl.BlockSpec((seq, V_HEAD_DIM), lambda b, h, i: (b, h)),
            pl.BlockSpec((tq, V_HEAD_DIM), lambda b, h, i: (b * nq + i, h)),
        ],
        out_specs=pl.BlockSpec((tq, V_HEAD_DIM), lambda b, h, i: (b * nq + i, h)),
        out_shape=jax.ShapeDtypeStruct((batch * seq, MLA_W), BF16),
        scratch_shapes=[pltpu.VMEM((tq, V_HEAD_DIM), F32), pltpu.VMEM((tq, LANES), F32),
                        pltpu.VMEM((tq, LANES), F32)],
        compiler_params=pltpu.CompilerParams(
            dimension_semantics=("parallel", "parallel", "arbitrary"), vmem_limit_bytes=VMEM_LIMIT),
        name="mla_attention",
    )(bound, qx, kx, v, gq)


def _out_kernel(main_ref, qm_ref, gm_ref, mkv_ref, gq_ref, gk_ref, x_ref, w_ref, o_ref, *, main_w):
    heads = []
    for h in range(N_MEM_HEADS):
        lo, hi = h * HEAD_DIM, (h + 1) * HEAD_DIM
        mk = mkv_ref[0, :, lo:hi]
        mk = (mk * _rms_scale(mk) * gk_ref[...]).astype(BF16)
        mv = mkv_ref[0, :, MEM_W + lo:MEM_W + hi].astype(BF16)
        q = qm_ref[:, lo:hi].astype(F32)
        q = (q * _rms_scale(q) * gq_ref[...]).astype(BF16)
        s = _dot_nt(q, mk) * HEAD_DIM ** -0.5
        p = jnp.exp(s - jnp.max(s, axis=-1, keepdims=True))
        mo = _dot(p.astype(BF16), mv) / jnp.sum(p, axis=-1, keepdims=True)
        heads.append((mo * _silu(gm_ref[:, lo:hi].astype(F32))).astype(BF16))
    acc = _dot(main_ref[...], w_ref[:main_w, :]) + _dot(jnp.concatenate(heads, axis=1), w_ref[main_w:, :])
    o_ref[...] = x_ref[...] + acc


def _mem_out_proj(main, side, qm_blk, gm_blk, mkv, g_q, g_k, x, w, seq):
    m, main_w = main.shape
    d = w.shape[1]
    mem_len = mkv.shape[1]
    tm = OUT_TM
    assert seq % tm == 0, (seq, tm)
    per_b = seq // tm
    return pl.pallas_call(
        functools.partial(_out_kernel, main_w=main_w),
        grid=(m // tm,),
        in_specs=[
            pl.BlockSpec((tm, main_w), lambda i: (i, 0)),
            pl.BlockSpec((tm, MEM_W), lambda i: (i, qm_blk)),
            pl.BlockSpec((tm, MEM_W), lambda i: (i, gm_blk)),
            pl.BlockSpec((1, mem_len, 2 * MEM_W), lambda i: (i // per_b, 0, 0)),
            pl.BlockSpec((1, HEAD_DIM), lambda i: (0, 0)),
            pl.BlockSpec((1, HEAD_DIM), lambda i: (0, 0)),
            pl.BlockSpec((tm, d), lambda i: (i, 0)),
            pl.BlockSpec((main_w + MEM_W, d), lambda i: (0, 0), pipeline_mode=pl.Buffered(1)),
        ],
        out_specs=pl.BlockSpec((tm, d), lambda i: (i, 0)),
        out_shape=jax.ShapeDtypeStruct((m, d), F32),
        compiler_params=pltpu.CompilerParams(
            dimension_semantics=("parallel",), vmem_limit_bytes=VMEM_LIMIT),
        name="mem_out_proj",
    )(main, side, side, mkv, g_q.reshape(1, HEAD_DIM), g_k.reshape(1, HEAD_DIM), x, w)


def _rope_tables(pos_ref, c_ref):
    ang = pos_ref[...].astype(F32) * c_ref[0:1, :]
    cosv, sinv = jnp.cos(ang), jnp.sin(ang)
    return cosv * c_ref[1:2, :], sinv * c_ref[2:3, :], sinv * c_ref[3:4, :]


def _rope(x, tables):
    c, s1, s2 = tables
    return x * c + pltpu.roll(x, LANES - QK_ROPE_DIM // 2, 1) * s1 + pltpu.roll(x, QK_ROPE_DIM // 2, 1) * s2


def _kv_kernel(x_ref, pos_ref, c_ref, gx_ref, wd_ref, gc_ref, wu_ref, gkn_ref, gkr_ref, kx_ref, v_ref):
    x = x_ref[...]
    h = (x * _rms_scale(x) * gx_ref[...]).astype(BF16)
    c = _dot(h, wd_ref[...])
    ckv = c[:, :KV_LORA_RANK]
    cn = (ckv * _rms_scale(ckv) * gc_ref[...]).astype(BF16)
    kv = _dot(cn, wu_ref[...])
    kr = c[:, KV_LORA_RANK:]
    kr = kr * _rms_scale(kr, QK_ROPE_DIM) * gkr_ref[...]
    k_rope = _rope(kr, _rope_tables(pos_ref, c_ref)).astype(BF16)
    for hd in range(N_MLA_HEADS):
        base = hd * QK_PAD
        kn = kv[:, base:base + QK_NOPE_DIM]
        kx_ref[:, base:base + QK_NOPE_DIM] = (kn * _rms_scale(kn) * gkn_ref[...]).astype(BF16)
        kx_ref[:, base + QK_NOPE_DIM:base + QK_PAD] = k_rope
        v_ref[:, hd * V_HEAD_DIM:(hd + 1) * V_HEAD_DIM] = kv[:, base + QK_NOPE_DIM:base + QK_PAD].astype(BF16)


def _q_kernel(x_ref, pos_ref, c_ref, gx_ref, win_ref, gl_ref, wuq_ref, gqn_ref, gqr_ref, qx_ref, side_ref):
    x = x_ref[...]
    h = (x * _rms_scale(x) * gx_ref[...]).astype(BF16)
    ql = _dot(h, win_ref[:, :Q_LORA_RANK])
    qn = (ql * _rms_scale(ql) * gl_ref[...]).astype(BF16)
    q = _dot(qn, wuq_ref[...])
    tables = _rope_tables(pos_ref, c_ref)
    for hd in range(N_MLA_HEADS):
        base = hd * QK_PAD
        qnope = q[:, base:base + QK_NOPE_DIM]
        qx_ref[:, base:base + QK_NOPE_DIM] = (qnope * _rms_scale(qnope) * gqn_ref[...] * MLA_SCALE).astype(BF16)
        qr = q[:, base + QK_NOPE_DIM:base + QK_PAD]
        qr = qr * _rms_scale(qr, QK_ROPE_DIM) * gqr_ref[...]
        qx_ref[:, base + QK_NOPE_DIM:base + QK_PAD] = (_rope(qr, tables) * MLA_SCALE).astype(BF16)
    side_ref[...] = _dot(h, win_ref[:, Q_LORA_RANK:]).astype(BF16)


def _row_call(body, x, pos, consts, small, outs, tm, name):
    m, d = x.shape
    assert m % tm == 0, (m, tm)
    resident = [pl.BlockSpec(a.shape, lambda i: (0, 0), pipeline_mode=pl.Buffered(1)) for a in (consts, *small)]
    return pl.pallas_call(
        body,
        grid=(m // tm,),
        in_specs=[pl.BlockSpec((tm, d), lambda i: (i, 0)), pl.BlockSpec((tm, 1), lambda i: (i, 0))] + resident,
        out_specs=[pl.BlockSpec((tm, w), lambda i: (i, 0)) for w in outs],
        out_shape=[jax.ShapeDtypeStruct((m, w), BF16) for w in outs],
        compiler_params=pltpu.CompilerParams(
            dimension_semantics=("parallel",), vmem_limit_bytes=VMEM_LIMIT),
        name=name,
    )(x, pos, consts, *small)


def _rope_consts():
    half = QK_ROPE_DIM // 2
    inv_freq = jnp.power(ROPE_THETA, -jnp.arange(0, QK_ROPE_DIM, 2, dtype=F32) / QK_ROPE_DIM)
    lane = np.arange(LANES)
    rows = jnp.zeros((8, LANES), F32)
    rows = rows.at[0, :QK_ROPE_DIM].set(jnp.concatenate([inv_freq, inv_freq]))
    rows = rows.at[1].set(jnp.asarray(lane < QK_ROPE_DIM, F32))
    rows = rows.at[2].set(jnp.asarray(-(lane < half).astype(np.float32)))
    rows = rows.at[3].set(jnp.asarray(((lane >= half) & (lane < QK_ROPE_DIM)).astype(np.float32)))
    return rows


def _pad_cols(a, width):
    return jnp.pad(a, [(0, 0)] * (a.ndim - 1) + [(0, width - a.shape[-1])])


def kernel(x, mem, positions, a_norm, a_w_in, a_w_out, kv_norm, w_dkv, g_ckv, w_ukv, g_k_nope, g_k_rope, b_norm, b_w_in, b_g_q_lat, b_w_uq, b_g_q_nope, b_g_q_rope, b_w_out, mem_norm, w_mem_kv, g_mem_q, g_mem_k):
    batch, seq, d = x.shape
    m = batch * seq
    mem_len = mem.shape[1]
    x2 = x.reshape(m, d)
    mem2 = mem.reshape(batch * mem_len, d)
    pos = positions.reshape(m, 1)
    consts = _rope_consts()
    row = lambda g: g.reshape(1, -1)

    def mem_kv(layer, w):
        mkv = _norm_matmul(mem2, mem_norm[layer], w, jnp.ones((2 * MEM_W,), F32), F32, *MEMKV_TILES)
        return mkv.reshape(batch, mem_len, 2 * MEM_W)

    a_in_w = a_w_in.shape[-1]
    qscale = jnp.ones((a_in_w,), F32).at[:SB_W].set(HEAD_DIM ** -0.5 * LOG2_E)
    pa = _norm_matmul(x2, a_norm[0], a_w_in[0].astype(BF16), qscale, BF16, INPROJ_TM, a_in_w // INPROJ_NSPLIT)
    sb, (a_wo, wd, wu, b_wi, wuq, b_wo, wm0, wm1) = _sb_attention(
        pa, batch, seq,
        (a_w_out[0], w_dkv, w_ukv, b_w_in[0], b_w_uq[0], b_w_out[0], w_mem_kv[0], w_mem_kv[1]))
    qm_blk = 4 * SB_W // MEM_W
    x2 = _mem_out_proj(sb, pa, qm_blk, qm_blk + 1, mem_kv(0, wm0), g_mem_q[0], g_mem_k[0], x2, a_wo, seq)

    kx, v = _row_call(
        _kv_kernel, x2, pos, consts,
        (row(kv_norm), _pad_cols(wd, KV_LORA_RANK + LANES), row(g_ckv), wu, row(g_k_nope),
         _pad_cols(row(g_k_rope), LANES)),
        (N_MLA_HEADS * QK_PAD, MLA_W), KV_TM, "mla_kv_side")

    wuq = wuq.reshape(Q_LORA_RANK, N_MLA_HEADS, QK_NOPE_DIM + QK_ROPE_DIM)
    wuq = _pad_cols(wuq, QK_PAD).reshape(Q_LORA_RANK, N_MLA_HEADS * QK_PAD)
    qx, side = _row_call(
        _q_kernel, x2, pos, consts,
        (row(b_norm[0]), b_wi, row(b_g_q_lat[0]), wuq, row(b_g_q_nope[0]), _pad_cols(row(b_g_q_rope[0]), LANES)),
        (N_MLA_HEADS * QK_PAD, MLA_W + 2 * MEM_W), Q_TM, "mla_q_side")
    bound = _mla_score_bound(b_g_q_nope[0], b_g_q_rope[0], g_k_nope, g_k_rope)
    att = _mla_attention(bound, qx, kx, v, side, batch, seq)
    qm_blk = MLA_W // MEM_W
    x2 = _mem_out_proj(att, side, qm_blk, qm_blk + 1, mem_kv(1, wm1), g_mem_q[1], g_mem_k[1], x2, b_wo, seq)
    return x2.reshape(batch, seq, d)
```

```python
import functools
import math

import jax
import jax.numpy as jnp
import numpy as np
from jax import lax
from jax.experimental import pallas as pl
from jax.experimental.pallas import tpu as pltpu

F32 = jnp.float32
BF16 = jnp.bfloat16

HEAD_DIM = 128
N_SB_HEADS = 12
N_MEM_HEADS = 4
N_MLA_HEADS = 12
Q_LORA_RANK = 512
KV_LORA_RANK = 512
QK_NOPE_DIM = 128
QK_ROPE_DIM = 64
V_HEAD_DIM = 128
ROPE_THETA = 10000.0
EPS = 1e-6
SB_W = N_SB_HEADS * HEAD_DIM
MEM_W = N_MEM_HEADS * HEAD_DIM
MLA_W = N_MLA_HEADS * V_HEAD_DIM
LOG2_E = math.log2(math.e)
LANES = 128
QK_PAD = 2 * LANES
VMEM_LIMIT = 56 * 1024 * 1024

INPROJ_TM, INPROJ_NSPLIT = 512, 2
MEMKV_TILES = (256, 512)
OUT_TM = 512
KV_TM = 1024
Q_TM = 512
SB_TQ = 4096
MLA_TQ, MLA_TK = 2048, 512
SB_SUB = 2 * LANES
EXP2_CLAMP = 126.0
SB_DEAD_LOG2 = 150.0
SB_NO_KEYS_LEFT = 1e30
SB_CAST_SLABS = 32
BF16_ROWS = 16
MLA_SAFE_BOUND = 60.0
MLA_SCALE = (QK_NOPE_DIM + QK_ROPE_DIM) ** -0.5 * LOG2_E
MLA_BOUND_SLACK = 1.02


def _rms_scale(x, width=None):
    ss = jnp.sum(x * x, axis=-1, keepdims=True)
    return lax.rsqrt(ss / (x.shape[-1] if width is None else width) + EPS)


def _silu(g):
    return g / (1.0 + jnp.exp(-g))


def _dot(a, b):
    return jnp.dot(a, b, preferred_element_type=F32)


def _dot_nt(a, b):
    return lax.dot_general(a, b, (((1,), (1,)), ((), ())), preferred_element_type=F32)


def _normmm_kernel(x_ref, g_ref, w_ref, cs_ref, o_ref, h_ref):
    @pl.when(pl.program_id(1) == 0)
    def _():
        x = x_ref[...]
        h_ref[...] = (x * _rms_scale(x) * g_ref[...]).astype(BF16)

    o_ref[...] = (_dot(h_ref[...], w_ref[...]) * cs_ref[...]).astype(o_ref.dtype)


def _norm_matmul(x, g, w, colscale, out_dtype, tm, tn, w_row_block=0):
    m, k = x.shape
    n = w.shape[1]
    assert m % tm == 0 and n % tn == 0 and w.shape[0] % k == 0, (m, n, tm, tn, w.shape)
    return pl.pallas_call(
        _normmm_kernel,
        grid=(m // tm, n // tn),
        in_specs=[
            pl.BlockSpec((tm, k), lambda i, j: (i, 0)),
            pl.BlockSpec((1, k), lambda i, j: (0, 0)),
            pl.BlockSpec((k, tn), lambda i, j: (w_row_block, j)),
            pl.BlockSpec((1, tn), lambda i, j: (0, j)),
        ],
        out_specs=pl.BlockSpec((tm, tn), lambda i, j: (i, j)),
        out_shape=jax.ShapeDtypeStruct((m, n), out_dtype),
        scratch_shapes=[pltpu.VMEM((tm, k), BF16)],
        compiler_params=pltpu.CompilerParams(
            dimension_semantics=("parallel", "arbitrary"), vmem_limit_bytes=VMEM_LIMIT),
        name="norm_matmul",
    )(x, g.reshape(1, k), w, colscale.reshape(1, n))


def _mask_top(x, keep, fill):
    n = keep.shape[0]
    top = jnp.where(keep, x[:n], fill)
    return top if x.shape[0] == n else jnp.concatenate([top, x[n:]], axis=0)


def _walk_key_blocks(block, i, nd):
    for d in reversed(range(nd)):
        block(i * nd + d, d, True)

    def body(j, carry):
        block(i * nd - 1 - j, 0, False)
        return carry

    lax.fori_loop(0, i * nd, body, 0)


def _sb_kernel(*refs, tq, n_cast):
    q_ref, k_ref, v_ref, g_ref, u_ref = refs[:5]
    o_ref = refs[5 + n_cast]
    acc_ref, carry_ref = refs[6 + 2 * n_cast:]
    for src_ref, dst_ref in zip(refs[5:5 + n_cast], refs[6 + n_cast:6 + 2 * n_cast]):
        dst_ref[...] = src_ref[...].astype(BF16)

    i = pl.program_id(2)
    nsub = tq // SB_SUB
    u = u_ref[...]
    row = lax.broadcasted_iota(jnp.int32, (SB_SUB, SB_SUB), 0)
    col = lax.broadcasted_iota(jnp.int32, (SB_SUB, SB_SUB), 1)
    causal = col < row
    acc_ref[...] = jnp.zeros_like(acc_ref)
    carry_ref[...] = jnp.zeros_like(carry_ref)

    def rows(r):
        return slice(r * SB_SUB, (r + 1) * SB_SUB)

    def per_sub_block(fn, x):
        return jnp.concatenate([fn(x[rows(r)]) for r in range(nsub)], axis=0)

    def step(o, diagonal):
        starts, zs = [], []
        for r in range(nsub):
            kb = i * nsub + r - o
            if not diagonal:
                carry_ref[rows(r), :] = jnp.where(kb >= 0, carry_ref[rows(r), :], SB_NO_KEYS_LEFT)
                kb = jnp.maximum(kb, 0)
            starts.append(pl.multiple_of(kb * SB_SUB, SB_SUB))
            zs.append(_dot_nt(q_ref[rows(r), :], k_ref[pl.ds(starts[r], SB_SUB), :]))
        z = jnp.concatenate(zs, axis=0)
        neg_log = jnp.maximum(z, jnp.log(1.0 + jnp.exp2(jnp.minimum(z, EXP2_CLAMP))) * LOG2_E)
        if diagonal:
            neg_log = per_sub_block(lambda t: jnp.where(causal, t, 0.0), neg_log)
        suffix = _dot(neg_log.astype(BF16), u)
        seen = carry_ref[...]
        a = jnp.exp2(z - neg_log - suffix - jnp.concatenate([seen] * (SB_SUB // LANES), axis=1))
        if diagonal:
            a = per_sub_block(lambda t: jnp.where(causal, t, 0.0), a)
        a = a.astype(BF16)
        for r in range(nsub):
            acc_ref[rows(r), :] += _dot(a[rows(r)], v_ref[pl.ds(starts[r], SB_SUB), :])
        carry_ref[...] = seen + jnp.sum(neg_log, axis=-1, keepdims=True)

    step(0, True)
    step(1, False)

    def more(state):
        o, lightest = state
        return jnp.logical_and(o < (i + 1) * nsub, lightest < SB_DEAD_LOG2)

    def advance(state):
        step(state[0], False)
        return state[0] + 1, jnp.min(carry_ref[...])

    lax.while_loop(more, advance, (2, jnp.min(carry_ref[...])))
    o_ref[...] = (acc_ref[...] * _silu(g_ref[...].astype(F32))).astype(o_ref.dtype)


def _sb_attention(pa, batch, seq, to_bf16):
    tq = SB_TQ
    assert seq % tq == 0 and tq % SB_SUB == 0, (seq, tq)
    nq = seq // tq
    h0 = N_SB_HEADS
    u = jnp.tril(jnp.ones((SB_SUB, SB_SUB), F32), -1).astype(BF16)
    assert batch * N_SB_HEADS * nq >= SB_CAST_SLABS

    def slab(w):
        assert w.shape[0] % (SB_CAST_SLABS * BF16_ROWS) == 0, w.shape
        step = lambda b, h, i: (jnp.minimum((b * N_SB_HEADS + h) * nq + i, SB_CAST_SLABS - 1), 0)
        return pl.BlockSpec((w.shape[0] // SB_CAST_SLABS, w.shape[1]), step)

    out, *cast = pl.pallas_call(
        functools.partial(_sb_kernel, tq=tq, n_cast=len(to_bf16)),
        grid=(batch, N_SB_HEADS, nq),
        in_specs=[
            pl.BlockSpec((tq, HEAD_DIM), lambda b, h, i: (b * nq + i, h)),
            pl.BlockSpec((seq, HEAD_DIM), lambda b, h, i: (b, h0 + h)),
            pl.BlockSpec((seq, HEAD_DIM), lambda b, h, i: (b, 2 * h0 + h)),
            pl.BlockSpec((tq, HEAD_DIM), lambda b, h, i: (b * nq + i, 3 * h0 + h)),
            pl.BlockSpec((SB_SUB, SB_SUB), lambda b, h, i: (0, 0)),
        ] + [slab(w) for w in to_bf16],
        out_specs=[pl.BlockSpec((tq, HEAD_DIM), lambda b, h, i: (b * nq + i, h))] + [slab(w) for w in to_bf16],
        out_shape=[jax.ShapeDtypeStruct((batch * seq, SB_W), BF16)]
        + [jax.ShapeDtypeStruct(w.shape, BF16) for w in to_bf16],
        scratch_shapes=[pltpu.VMEM((tq, HEAD_DIM), F32), pltpu.VMEM((tq, LANES), F32)],
        compiler_params=pltpu.CompilerParams(
            dimension_semantics=("arbitrary", "arbitrary", "arbitrary"), vmem_limit_bytes=VMEM_LIMIT),
        name="sb_attention",
    )(pa, pa, pa, pa, u, *to_bf16)
    return out, cast


def _mla_kernel(bound_ref, q_ref, k_ref, v_ref, g_ref, o_ref, acc_ref, m_ref, den_ref, *, tq, tk):
    i = pl.program_id(2)
    row = lax.broadcasted_iota(jnp.int32, (tk, tk), 0)
    col = lax.broadcasted_iota(jnp.int32, (tk, tk), 1)
    causal = col <= row
    nlane = tk // LANES
    nd = tq // tk
    bound = bound_ref[0]

    def scores(kb, d, diagonal):
        start = pl.multiple_of(kb * tk, tk)
        s = _dot_nt(q_ref[d * tk:, :], k_ref[pl.ds(start, tk), :])
        return _mask_top(s, causal, -jnp.inf) if diagonal else s

    def finish(den):
        o_ref[...] = (acc_ref[...] / den * _silu(g_ref[...].astype(F32))).astype(o_ref.dtype)

    def add(xs):
        return functools.reduce(lambda a, b: a + b, xs)

    def bounded():
        def tile(kb, d, diagonal):
            s = scores(kb, d, diagonal)
            ps = [jnp.exp2(s[:, c * LANES:(c + 1) * LANES] - bound) for c in range(nlane)]
            p = jnp.concatenate(ps, axis=1).astype(BF16)
            return add(ps), _dot(p, v_ref[pl.ds(pl.multiple_of(kb * tk, tk), tk), :])

        dens, pvs = zip(*[tile(i * nd + d, d, True) for d in range(nd)])
        for r in range(nd):
            rows = slice(r * tk, (r + 1) * tk)
            den_ref[rows, :] = add([dens[d][(r - d) * tk:(r - d + 1) * tk] for d in range(r + 1)])
            acc_ref[rows, :] = add([pvs[d][(r - d) * tk:(r - d + 1) * tk] for d in range(r + 1)])

        def older(j, carry):
            dens, pvs = zip(*[tile((i - j) * nd - 1 - u, 0, False) for u in range(nd)])
            den_ref[...] += add(dens)
            acc_ref[...] += add(pvs)
            return carry

        lax.fori_loop(0, i, older, 0)
        finish(jnp.sum(den_ref[...], axis=-1, keepdims=True))

    def online():
        acc_ref[...] = jnp.zeros_like(acc_ref)
        den_ref[...] = jnp.zeros_like(den_ref)
        m_ref[...] = jnp.full_like(m_ref, -jnp.inf)

        def block(kb, d, diagonal):
            r0 = d * tk
            s = scores(kb, d, diagonal)
            m_old = m_ref[r0:, :]
            m_new = jnp.maximum(m_old, jnp.max(s, axis=-1, keepdims=True))
            alpha = jnp.exp2(m_old - m_new)
            p = jnp.concatenate([jnp.exp2(s[:, c * LANES:(c + 1) * LANES] - m_new) for c in range(nlane)], axis=1)
            den_ref[r0:, :] = alpha * den_ref[r0:, :] + jnp.sum(p, axis=-1, keepdims=True)
            pv = _dot(p.astype(BF16), v_ref[pl.ds(pl.multiple_of(kb * tk, tk), tk), :])
            acc_ref[r0:, :] = alpha * acc_ref[r0:, :] + pv
            m_ref[r0:, :] = m_new

        _walk_key_blocks(block, i, nd)
        finish(den_ref[...])

    lax.cond(bound <= MLA_SAFE_BOUND, bounded, online)


def _mla_score_bound(g_q_nope, g_q_rope, g_k_nope, g_k_rope):
    def norm(g_nope, g_rope):
        return jnp.sqrt(QK_NOPE_DIM * jnp.max(g_nope * g_nope) + QK_ROPE_DIM * jnp.max(g_rope * g_rope))

    return (MLA_BOUND_SLACK * MLA_SCALE * norm(g_q_nope, g_q_rope) * norm(g_k_nope, g_k_rope)).reshape(1)


def _mla_attention(bound, qx, kx, v, gq, batch, seq):
    tq, tk = MLA_TQ, MLA_TK
    assert seq % tq == 0 and tq % tk == 0 and tk % LANES == 0, (seq, tq, tk)
    nq = seq // tq
    return pl.pallas_call(
        functools.partial(_mla_kernel, tq=tq, tk=tk),
        grid=(batch, N_MLA_HEADS, nq),
        in_specs=[
            pl.BlockSpec(memory_space=pltpu.SMEM),
            pl.BlockSpec((tq, QK_PAD), lambda b, h, i: (b * nq + i, h)),
            pl.BlockSpec((seq, QK_PAD), lambda b, h, i: (b, h)),
            pl.BlockSpec((seq, V_HEAD_DIM), lambda b, h, i: (b, h)),
            pl.BlockSpec((tq, V_HEAD_DIM), lambda b, h, i: (b * nq + i, h)),
        ],
        out_specs=pl.BlockSpec((tq, V_HEAD_DIM), lambda b, h, i: (b * nq + i, h)),
        out_shape=jax.ShapeDtypeStruct((batch * seq, MLA_W), BF16),
        scratch_shapes=[pltpu.VMEM((tq, V_HEAD_DIM), F32), pltpu.VMEM((tq, LANES), F32),
                        pltpu.VMEM((tq, LANES), F32)],
        compiler_params=pltpu.CompilerParams(
            dimension_semantics=("parallel", "parallel", "arbitrary"), vmem_limit_bytes=VMEM_LIMIT),
        name="mla_attention",
    )(bound, qx, kx, v, gq)


def _out_kernel(main_ref, qm_ref, gm_ref, mkv_ref, gq_ref, gk_ref, x_ref, w_ref, o_ref, *, main_w):
    heads = []
    for h in range(N_MEM_HEADS):
        lo, hi = h * HEAD_DIM, (h + 1) * HEAD_DIM
        mk = mkv_ref[0, :, lo:hi]
        mk = (mk * _rms_scale(mk) * gk_ref[...]).astype(BF16)
        mv = mkv_ref[0, :, MEM_W + lo:MEM_W + hi].astype(BF16)
        q = qm_ref[:, lo:hi].astype(F32)
        q = (q * _rms_scale(q) * gq_ref[...]).astype(BF16)
        s = _dot_nt(q, mk) * HEAD_DIM ** -0.5
        p = jnp.exp(s - jnp.max(s, axis=-1, keepdims=True))
        mo = _dot(p.astype(BF16), mv) / jnp.sum(p, axis=-1, keepdims=True)
        heads.append((mo * _silu(gm_ref[:, lo:hi].astype(F32))).astype(BF16))
    acc = _dot(main_ref[...], w_ref[:main_w, :]) + _dot(jnp.concatenate(heads, axis=1), w_ref[main_w:, :])
    o_ref[...] = x_ref[...] + acc


def _mem_out_proj(main, side, qm_blk, gm_blk, mkv, g_q, g_k, x, w, seq):
    m, main_w = main.shape
    d = w.shape[1]
    mem_len = mkv.shape[1]
    tm = OUT_TM
    assert seq % tm == 0, (seq, tm)
    per_b = seq // tm
    return pl.pallas_call(
        functools.partial(_out_kernel, main_w=main_w),
        grid=(m // tm,),
        in_specs=[
            pl.BlockSpec((tm, main_w), lambda i: (i, 0)),
            pl.BlockSpec((tm, MEM_W), lambda i: (i, qm_blk)),
            pl.BlockSpec((tm, MEM_W), lambda i: (i, gm_blk)),
            pl.BlockSpec((1, mem_len, 2 * MEM_W), lambda i: (i // per_b, 0, 0)),
            pl.BlockSpec((1, HEAD_DIM), lambda i: (0, 0)),
            pl.BlockSpec((1, HEAD_DIM), lambda i: (0, 0)),
            pl.BlockSpec((tm, d), lambda i: (i, 0)),
            pl.BlockSpec((main_w + MEM_W, d), lambda i: (0, 0), pipeline_mode=pl.Buffered(1)),
        ],
        out_specs=pl.BlockSpec((tm, d), lambda i: (i, 0)),
        out_shape=jax.ShapeDtypeStruct((m, d), F32),
        compiler_params=pltpu.CompilerParams(
            dimension_semantics=("parallel",), vmem_limit_bytes=VMEM_LIMIT),
        name="mem_out_proj",
    )(main, side, side, mkv, g_q.reshape(1, HEAD_DIM), g_k.reshape(1, HEAD_DIM), x, w)


def _rope_tables(pos_ref, c_ref):
    ang = pos_ref[...].astype(F32) * c_ref[0:1, :]
    cosv, sinv = jnp.cos(ang), jnp.sin(ang)
    return cosv * c_ref[1:2, :], sinv * c_ref[2:3, :], sinv * c_ref[3:4, :]


def _rope(x, tables):
    c, s1, s2 = tables
    return x * c + pltpu.roll(x, LANES - QK_ROPE_DIM // 2, 1) * s1 + pltpu.roll(x, QK_ROPE_DIM // 2, 1) * s2


def _kv_kernel(x_ref, pos_ref, c_ref, gx_ref, wd_ref, gc_ref, wu_ref, gkn_ref, gkr_ref, kx_ref, v_ref):
    x = x_ref[...]
    h = (x * _rms_scale(x) * gx_ref[...]).astype(BF16)
    c = _dot(h, wd_ref[...])
    ckv = c[:, :KV_LORA_RANK]
    cn = (ckv * _rms_scale(ckv) * gc_ref[...]).astype(BF16)
    kv = _dot(cn, wu_ref[...])
    kr = c[:, KV_LORA_RANK:]
    kr = kr * _rms_scale(kr, QK_ROPE_DIM) * gkr_ref[...]
    k_rope = _rope(kr, _rope_tables(pos_ref, c_ref)).astype(BF16)
    for hd in range(N_MLA_HEADS):
        base = hd * QK_PAD
        kn = kv[:, base:base + QK_NOPE_DIM]
        kx_ref[:, base:base + QK_NOPE_DIM] = (kn * _rms_scale(kn) * gkn_ref[...]).astype(BF16)
        kx_ref[:, base + QK_NOPE_DIM:base + QK_PAD] = k_rope
        v_ref[:, hd * V_HEAD_DIM:(hd + 1) * V_HEAD_DIM] = kv[:, base + QK_NOPE_DIM:base + QK_PAD].astype(BF16)


def _q_kernel(x_ref, pos_ref, c_ref, gx_ref, win_ref, gl_ref, wuq_ref, gqn_ref, gqr_ref, qx_ref, side_ref):
    x = x_ref[...]
    h = (x * _rms_scale(x) * gx_ref[...]).astype(BF16)
    ql = _dot(h, win_ref[:, :Q_LORA_RANK])
    qn = (ql * _rms_scale(ql) * gl_ref[...]).astype(BF16)
    q = _dot(qn, wuq_ref[...])
    tables = _rope_tables(pos_ref, c_ref)
    for hd in range(N_MLA_HEADS):
        base = hd * QK_PAD
        qnope = q[:, base:base + QK_NOPE_DIM]
        qx_ref[:, base:base + QK_NOPE_DIM] = (qnope * _rms_scale(qnope) * gqn_ref[...] * MLA_SCALE).astype(BF16)
        qr = q[:, base + QK_NOPE_DIM:base + QK_PAD]
        qr = qr * _rms_scale(qr, QK_ROPE_DIM) * gqr_ref[...]
        qx_ref[:, base + QK_NOPE_DIM:base + QK_PAD] = (_rope(qr, tables) * MLA_SCALE).astype(BF16)
    side_ref[...] = _dot(h, win_ref[:, Q_LORA_RANK:]).astype(BF16)


def _row_call(body, x, pos, consts, small, outs, tm, name):
    m, d = x.shape
    assert m % tm == 0, (m, tm)
    resident = [pl.BlockSpec(a.shape, lambda i: (0, 0), pipeline_mode=pl.Buffered(1)) for a in (consts, *small)]
    return pl.pallas_call(
        body,
        grid=(m // tm,),
        in_specs=[pl.BlockSpec((tm, d), lambda i: (i, 0)), pl.BlockSpec((tm, 1), lambda i: (i, 0))] + resident,
        out_specs=[pl.BlockSpec((tm, w), lambda i: (i, 0)) for w in outs],
        out_shape=[jax.ShapeDtypeStruct((m, w), BF16) for w in outs],
        compiler_params=pltpu.CompilerParams(
            dimension_semantics=("parallel",), vmem_limit_bytes=VMEM_LIMIT),
        name=name,
    )(x, pos, consts, *small)


def _rope_consts():
    half = QK_ROPE_DIM // 2
    inv_freq = jnp.power(ROPE_THETA, -jnp.arange(0, QK_ROPE_DIM, 2, dtype=F32) / QK_ROPE_DIM)
    lane = np.arange(LANES)
    rows = jnp.zeros((8, LANES), F32)
    rows = rows.at[0, :QK_ROPE_DIM].set(jnp.concatenate([inv_freq, inv_freq]))
    rows = rows.at[1].set(jnp.asarray(lane < QK_ROPE_DIM, F32))
    rows = rows.at[2].set(jnp.asarray(-(lane < half).astype(np.float32)))
    rows = rows.at[3].set(jnp.asarray(((lane >= half) & (lane < QK_ROPE_DIM)).astype(np.float32)))
    return rows


def _pad_cols(a, width):
    return jnp.pad(a, [(0, 0)] * (a.ndim - 1) + [(0, width - a.shape[-1])])


def kernel(x, mem, positions, a_norm, a_w_in, a_w_out, kv_norm, w_dkv, g_ckv, w_ukv, g_k_nope, g_k_rope, b_norm, b_w_in, b_g_q_lat, b_w_uq, b_g_q_nope, b_g_q_rope, b_w_out, mem_norm, w_mem_kv, g_mem_q, g_mem_k):
    batch, seq, d = x.shape
    m = batch * seq
    mem_len = mem.shape[1]
    x2 = x.reshape(m, d)
    mem2 = mem.reshape(batch * mem_len, d)
    pos = positions.reshape(m, 1)
    consts = _rope_consts()
    row = lambda g: g.reshape(1, -1)

    def only(w):
        assert w.shape[0] == 1, w.shape
        return w.reshape(w.shape[1:])

    def mem_kv(layer, w_layers):
        mkv = _norm_matmul(mem2, mem_norm[layer], w_layers, jnp.ones((2 * MEM_W,), F32), F32, *MEMKV_TILES,
                           w_row_block=layer)
        return mkv.reshape(batch, mem_len, 2 * MEM_W)

    a_in_w = a_w_in.shape[-1]
    qscale = jnp.ones((a_in_w,), F32).at[:SB_W].set(HEAD_DIM ** -0.5 * LOG2_E)
    pa = _norm_matmul(x2, a_norm[0], only(a_w_in).astype(BF16), qscale, BF16, INPROJ_TM, a_in_w // INPROJ_NSPLIT)
    sb, (a_wo, wd, wu, b_wi, wuq, b_wo, wm) = _sb_attention(
        pa, batch, seq,
        (only(a_w_out), w_dkv, w_ukv, only(b_w_in), only(b_w_uq), only(b_w_out), w_mem_kv.reshape(-1, 2 * MEM_W)))
    qm_blk = 4 * SB_W // MEM_W
    x2 = _mem_out_proj(sb, pa, qm_blk, qm_blk + 1, mem_kv(0, wm), g_mem_q[0], g_mem_k[0], x2, a_wo, seq)

    kx, v = _row_call(
        _kv_kernel, x2, pos, consts,
        (row(kv_norm), _pad_cols(wd, KV_LORA_RANK + LANES), row(g_ckv), wu, row(g_k_nope),
         _pad_cols(row(g_k_rope), LANES)),
        (N_MLA_HEADS * QK_PAD, MLA_W), KV_TM, "mla_kv_side")

    wuq = wuq.reshape(Q_LORA_RANK, N_MLA_HEADS, QK_NOPE_DIM + QK_ROPE_DIM)
    wuq = _pad_cols(wuq, QK_PAD).reshape(Q_LORA_RANK, N_MLA_HEADS * QK_PAD)
    qx, side = _row_call(
        _q_kernel, x2, pos, consts,
        (row(b_norm[0]), b_wi, row(b_g_q_lat[0]), wuq, row(b_g_q_nope[0]), _pad_cols(row(b_g_q_rope[0]), LANES)),
        (N_MLA_HEADS * QK_PAD, MLA_W + 2 * MEM_W), Q_TM, "mla_q_side")
    bound = _mla_score_bound(b_g_q_nope[0], b_g_q_rope[0], g_k_nope, g_k_rope)
    att = _mla_attention(bound, qx, kx, v, side, batch, seq)
    qm_blk = MLA_W // MEM_W
    x2 = _mem_out_proj(att, side, qm_blk, qm_blk + 1, mem_kv(1, wm), g_mem_q[1], g_mem_k[1], x2, b_wo, seq)
    return x2.reshape(batch, seq, d)
```

```python
import functools
import math

import jax
import jax.numpy as jnp
import numpy as np
from jax import lax
from jax.experimental import pallas as pl
from jax.experimental.pallas import tpu as pltpu

F32 = jnp.float32
BF16 = jnp.bfloat16

HEAD_DIM = 128
N_SB_HEADS = 12
N_MEM_HEADS = 4
N_MLA_HEADS = 12
Q_LORA_RANK = 512
KV_LORA_RANK = 512
QK_NOPE_DIM = 128
QK_ROPE_DIM = 64
V_HEAD_DIM = 128
ROPE_THETA = 10000.0
EPS = 1e-6
SB_W = N_SB_HEADS * HEAD_DIM
MEM_W = N_MEM_HEADS * HEAD_DIM
MLA_W = N_MLA_HEADS * V_HEAD_DIM
LOG2_E = math.log2(math.e)
LANES = 128
QK_PAD = 2 * LANES
VMEM_LIMIT = 56 * 1024 * 1024

INPROJ_TM, INPROJ_NSPLIT = 512, 2
MEMKV_TILES = (512, 1024)
OUT_TM = 512
KV_TM = 1024
Q_TM = 512
SB_TQ = 4096
MLA_TQ, MLA_TK = 2048, 512
SB_SUB = 2 * LANES
EXP2_CLAMP = 126.0
SB_DEAD_LOG2 = 150.0
SB_NO_KEYS_LEFT = 1e30
SB_CAST_SLABS = 32
BF16_ROWS = 16
F32_ROWS = 8
MLA_SAFE_BOUND = 60.0
MLA_SCALE = (QK_NOPE_DIM + QK_ROPE_DIM) ** -0.5 * LOG2_E
MLA_BOUND_SLACK = 1.02


def _rms_scale(x, width=None):
    ss = jnp.sum(x * x, axis=-1, keepdims=True)
    return lax.rsqrt(ss / (x.shape[-1] if width is None else width) + EPS)


def _silu(g):
    return g / (1.0 + jnp.exp(-g))


def _dot(a, b):
    return jnp.dot(a, b, preferred_element_type=F32)


def _dot_nt(a, b):
    return lax.dot_general(a, b, (((1,), (1,)), ((), ())), preferred_element_type=F32)


def _normmm_kernel(x_ref, g_ref, w_ref, cs_ref, o_ref, h_ref):
    @pl.when(pl.program_id(1) == 0)
    def _():
        x = x_ref[...]
        h_ref[...] = (x * _rms_scale(x) * g_ref[...]).astype(BF16)

    o_ref[...] = (_dot(h_ref[...], w_ref[...]) * cs_ref[...]).astype(o_ref.dtype)


def _norm_matmul(x, g, w, colscale, out_dtype, tm, tn, w_row_block=0):
    m, k = x.shape
    n = w.shape[1]
    assert m % tm == 0 and n % tn == 0 and w.shape[0] % k == 0, (m, n, tm, tn, w.shape)
    return pl.pallas_call(
        _normmm_kernel,
        grid=(m // tm, n // tn),
        in_specs=[
            pl.BlockSpec((tm, k), lambda i, j: (i, 0)),
            pl.BlockSpec((1, k), lambda i, j: (0, 0)),
            pl.BlockSpec((k, tn), lambda i, j: (w_row_block, j)),
            pl.BlockSpec((1, tn), lambda i, j: (0, j)),
        ],
        out_specs=pl.BlockSpec((tm, tn), lambda i, j: (i, j)),
        out_shape=jax.ShapeDtypeStruct((m, n), out_dtype),
        scratch_shapes=[pltpu.VMEM((tm, k), BF16)],
        compiler_params=pltpu.CompilerParams(
            dimension_semantics=("parallel", "arbitrary"), vmem_limit_bytes=VMEM_LIMIT),
        name="norm_matmul",
    )(x, g.reshape(1, k), w, colscale.reshape(1, n))


def _mask_top(x, keep, fill):
    n = keep.shape[0]
    top = jnp.where(keep, x[:n], fill)
    return top if x.shape[0] == n else jnp.concatenate([top, x[n:]], axis=0)


def _walk_key_blocks(block, i, nd):
    for d in reversed(range(nd)):
        block(i * nd + d, d, True)

    def body(j, carry):
        block(i * nd - 1 - j, 0, False)
        return carry

    lax.fori_loop(0, i * nd, body, 0)


def _sb_kernel(*refs, tq, n_cast):
    q_ref, k_ref, v_ref, g_ref, u_ref = refs[:5]
    o_ref = refs[5 + n_cast]
    acc_ref, carry_ref = refs[6 + 2 * n_cast:]
    for src_ref, dst_ref in zip(refs[5:5 + n_cast], refs[6 + n_cast:6 + 2 * n_cast]):
        dst_ref[...] = src_ref[...].astype(BF16)

    i = pl.program_id(2)
    nsub = tq // SB_SUB
    u = u_ref[...]
    row = lax.broadcasted_iota(jnp.int32, (SB_SUB, SB_SUB), 0)
    col = lax.broadcasted_iota(jnp.int32, (SB_SUB, SB_SUB), 1)
    causal = col < row
    acc_ref[...] = jnp.zeros_like(acc_ref)
    carry_ref[...] = jnp.zeros_like(carry_ref)

    def rows(r):
        return slice(r * SB_SUB, (r + 1) * SB_SUB)

    def per_sub_block(fn, x):
        return jnp.concatenate([fn(x[rows(r)]) for r in range(nsub)], axis=0)

    def step(o, diagonal):
        starts, zs = [], []
        for r in range(nsub):
            kb = i * nsub + r - o
            if not diagonal:
                carry_ref[rows(r), :] = jnp.where(kb >= 0, carry_ref[rows(r), :], SB_NO_KEYS_LEFT)
                kb = jnp.maximum(kb, 0)
            starts.append(pl.multiple_of(kb * SB_SUB, SB_SUB))
            zs.append(_dot_nt(q_ref[rows(r), :], k_ref[pl.ds(starts[r], SB_SUB), :]))
        z = jnp.concatenate(zs, axis=0)
        neg_log = jnp.maximum(z, jnp.log(1.0 + jnp.exp2(jnp.minimum(z, EXP2_CLAMP))) * LOG2_E)
        if diagonal:
            neg_log = per_sub_block(lambda t: jnp.where(causal, t, 0.0), neg_log)
        suffix = _dot(neg_log.astype(BF16), u)
        seen = carry_ref[...]
        a = jnp.exp2(z - neg_log - suffix - jnp.concatenate([seen] * (SB_SUB // LANES), axis=1))
        if diagonal:
            a = per_sub_block(lambda t: jnp.where(causal, t, 0.0), a)
        a = a.astype(BF16)
        for r in range(nsub):
            acc_ref[rows(r), :] += _dot(a[rows(r)], v_ref[pl.ds(starts[r], SB_SUB), :])
        carry_ref[...] = seen + jnp.sum(neg_log, axis=-1, keepdims=True)

    step(0, True)
    step(1, False)

    def more(state):
        o, lightest = state
        return jnp.logical_and(o < (i + 1) * nsub, lightest < SB_DEAD_LOG2)

    def advance(state):
        step(state[0], False)
        return state[0] + 1, jnp.min(carry_ref[...])

    lax.while_loop(more, advance, (2, jnp.min(carry_ref[...])))
    o_ref[...] = (acc_ref[...] * _silu(g_ref[...].astype(F32))).astype(o_ref.dtype)


def _sb_attention(pa, batch, seq, to_bf16):
    tq = SB_TQ
    assert seq % tq == 0 and tq % SB_SUB == 0, (seq, tq)
    nq = seq // tq
    h0 = N_SB_HEADS
    u = jnp.tril(jnp.ones((SB_SUB, SB_SUB), F32), -1).astype(BF16)
    assert batch * N_SB_HEADS * nq >= SB_CAST_SLABS

    def slab(w):
        assert w.shape[0] % (SB_CAST_SLABS * BF16_ROWS) == 0, w.shape
        step = lambda b, h, i: (jnp.minimum((b * N_SB_HEADS + h) * nq + i, SB_CAST_SLABS - 1), 0)
        return pl.BlockSpec((w.shape[0] // SB_CAST_SLABS, w.shape[1]), step)

    out, *cast = pl.pallas_call(
        functools.partial(_sb_kernel, tq=tq, n_cast=len(to_bf16)),
        grid=(batch, N_SB_HEADS, nq),
        in_specs=[
            pl.BlockSpec((tq, HEAD_DIM), lambda b, h, i: (b * nq + i, h)),
            pl.BlockSpec((seq, HEAD_DIM), lambda b, h, i: (b, h0 + h)),
            pl.BlockSpec((seq, HEAD_DIM), lambda b, h, i: (b, 2 * h0 + h)),
            pl.BlockSpec((tq, HEAD_DIM), lambda b, h, i: (b * nq + i, 3 * h0 + h)),
            pl.BlockSpec((SB_SUB, SB_SUB), lambda b, h, i: (0, 0)),
        ] + [slab(w) for w in to_bf16],
        out_specs=[pl.BlockSpec((tq, HEAD_DIM), lambda b, h, i: (b * nq + i, h))] + [slab(w) for w in to_bf16],
        out_shape=[jax.ShapeDtypeStruct((batch * seq, SB_W), BF16)]
        + [jax.ShapeDtypeStruct(w.shape, BF16) for w in to_bf16],
        scratch_shapes=[pltpu.VMEM((tq, HEAD_DIM), F32), pltpu.VMEM((tq, LANES), F32)],
        compiler_params=pltpu.CompilerParams(
            dimension_semantics=("arbitrary", "arbitrary", "arbitrary"), vmem_limit_bytes=VMEM_LIMIT),
        name="sb_attention",
    )(pa, pa, pa, pa, u, *to_bf16)
    return out, cast


def _mla_kernel(bound_ref, q_ref, k_ref, v_ref, g_ref, o_ref, acc_ref, m_ref, den_ref, *, tq, tk):
    i = pl.program_id(2)
    row = lax.broadcasted_iota(jnp.int32, (tk, tk), 0)
    col = lax.broadcasted_iota(jnp.int32, (tk, tk), 1)
    causal = col <= row
    nlane = tk // LANES
    nd = tq // tk
    bound = bound_ref[0]

    def scores(kb, d, diagonal):
        start = pl.multiple_of(kb * tk, tk)
        s = _dot_nt(q_ref[d * tk:, :], k_ref[pl.ds(start, tk), :])
        return _mask_top(s, causal, -jnp.inf) if diagonal else s

    def finish(den):
        o_ref[...] = (acc_ref[...] / den * _silu(g_ref[...].astype(F32))).astype(o_ref.dtype)

    def add(xs):
        return functools.reduce(lambda a, b: a + b, xs)

    def bounded():
        def tile(kb, d, diagonal):
            s = scores(kb, d, diagonal)
            ps = [jnp.exp2(s[:, c * LANES:(c + 1) * LANES] - bound) for c in range(nlane)]
            p = jnp.concatenate(ps, axis=1).astype(BF16)
            return add(ps), _dot(p, v_ref[pl.ds(pl.multiple_of(kb * tk, tk), tk), :])

        dens, pvs = zip(*[tile(i * nd + d, d, True) for d in range(nd)])
        for r in range(nd):
            rows = slice(r * tk, (r + 1) * tk)
            den_ref[rows, :] = add([dens[d][(r - d) * tk:(r - d + 1) * tk] for d in range(r + 1)])
            acc_ref[rows, :] = add([pvs[d][(r - d) * tk:(r - d + 1) * tk] for d in range(r + 1)])

        def older(j, carry):
            dens, pvs = zip(*[tile((i - j) * nd - 1 - u, 0, False) for u in range(nd)])
            den_ref[...] += add(dens)
            acc_ref[...] += add(pvs)
            return carry

        lax.fori_loop(0, i, older, 0)
        finish(jnp.sum(den_ref[...], axis=-1, keepdims=True))

    def online():
        acc_ref[...] = jnp.zeros_like(acc_ref)
        den_ref[...] = jnp.zeros_like(den_ref)
        m_ref[...] = jnp.full_like(m_ref, -jnp.inf)

        def block(kb, d, diagonal):
            r0 = d * tk
            s = scores(kb, d, diagonal)
            m_old = m_ref[r0:, :]
            m_new = jnp.maximum(m_old, jnp.max(s, axis=-1, keepdims=True))
            alpha = jnp.exp2(m_old - m_new)
            p = jnp.concatenate([jnp.exp2(s[:, c * LANES:(c + 1) * LANES] - m_new) for c in range(nlane)], axis=1)
            den_ref[r0:, :] = alpha * den_ref[r0:, :] + jnp.sum(p, axis=-1, keepdims=True)
            pv = _dot(p.astype(BF16), v_ref[pl.ds(pl.multiple_of(kb * tk, tk), tk), :])
            acc_ref[r0:, :] = alpha * acc_ref[r0:, :] + pv
            m_ref[r0:, :] = m_new

        _walk_key_blocks(block, i, nd)
        finish(den_ref[...])

    lax.cond(bound <= MLA_SAFE_BOUND, bounded, online)


def _mla_score_bound(g_q_nope, g_q_rope, g_k_nope, g_k_rope):
    def norm(g_nope, g_rope):
        return jnp.sqrt(QK_NOPE_DIM * jnp.max(g_nope * g_nope) + QK_ROPE_DIM * jnp.max(g_rope * g_rope))

    return (MLA_BOUND_SLACK * MLA_SCALE * norm(g_q_nope, g_q_rope) * norm(g_k_nope, g_k_rope)).reshape(1)


def _mla_attention(bound, qx, kx, v, gq, batch, seq):
    tq, tk = MLA_TQ, MLA_TK
    assert seq % tq == 0 and tq % tk == 0 and tk % LANES == 0, (seq, tq, tk)
    nq = seq // tq
    return pl.pallas_call(
        functools.partial(_mla_kernel, tq=tq, tk=tk),
        grid=(batch, N_MLA_HEADS, nq),
        in_specs=[
            pl.BlockSpec(memory_space=pltpu.SMEM),
            pl.BlockSpec((tq, QK_PAD), lambda b, h, i: (b * nq + i, h)),
            pl.BlockSpec((seq, QK_PAD), lambda b, h, i: (b, h)),
            pl.BlockSpec((seq, V_HEAD_DIM), lambda b, h, i: (b, h)),
            pl.BlockSpec((tq, V_HEAD_DIM), lambda b, h, i: (b * nq + i, h)),
        ],
        out_specs=pl.BlockSpec((tq, V_HEAD_DIM), lambda b, h, i: (b * nq + i, h)),
        out_shape=jax.ShapeDtypeStruct((batch * seq, MLA_W), BF16),
        scratch_shapes=[pltpu.VMEM((tq, V_HEAD_DIM), F32), pltpu.VMEM((tq, LANES), F32),
                        pltpu.VMEM((tq, LANES), F32)],
        compiler_params=pltpu.CompilerParams(
            dimension_semantics=("parallel", "parallel", "arbitrary"), vmem_limit_bytes=VMEM_LIMIT),
        name="mla_attention",
    )(bound, qx, kx, v, gq)


def _out_kernel(main_ref, qm_ref, gm_ref, mkv_ref, gq_ref, gk_ref, x_ref, w_ref, o_ref, *, main_w):
    heads = []
    for h in range(N_MEM_HEADS):
        lo, hi = h * HEAD_DIM, (h + 1) * HEAD_DIM
        mk = mkv_ref[0, :, lo:hi]
        mk = (mk * _rms_scale(mk) * gk_ref[...]).astype(BF16)
        mv = mkv_ref[0, :, MEM_W + lo:MEM_W + hi].astype(BF16)
        q = qm_ref[:, lo:hi].astype(F32)
        q = (q * _rms_scale(q) * gq_ref[...]).astype(BF16)
        s = _dot_nt(q, mk) * HEAD_DIM ** -0.5
        p = jnp.exp(s - jnp.max(s, axis=-1, keepdims=True))
        mo = _dot(p.astype(BF16), mv) / jnp.sum(p, axis=-1, keepdims=True)
        heads.append((mo * _silu(gm_ref[:, lo:hi].astype(F32))).astype(BF16))
    acc = _dot(main_ref[...], w_ref[:main_w, :]) + _dot(jnp.concatenate(heads, axis=1), w_ref[main_w:, :])
    o_ref[...] = x_ref[...] + acc


def _mem_out_proj(main, side, qm_blk, gm_blk, mkv, g_q, g_k, x, w, seq):
    m, main_w = main.shape
    d = w.shape[1]
    mem_len = mkv.shape[1]
    tm = OUT_TM
    assert seq % tm == 0, (seq, tm)
    per_b = seq // tm
    return pl.pallas_call(
        functools.partial(_out_kernel, main_w=main_w),
        grid=(m // tm,),
        in_specs=[
            pl.BlockSpec((tm, main_w), lambda i: (i, 0)),
            pl.BlockSpec((tm, MEM_W), lambda i: (i, qm_blk)),
            pl.BlockSpec((tm, MEM_W), lambda i: (i, gm_blk)),
            pl.BlockSpec((1, mem_len, 2 * MEM_W), lambda i: (i // per_b, 0, 0)),
            pl.BlockSpec((1, HEAD_DIM), lambda i: (0, 0)),
            pl.BlockSpec((1, HEAD_DIM), lambda i: (0, 0)),
            pl.BlockSpec((tm, d), lambda i: (i, 0)),
            pl.BlockSpec((main_w + MEM_W, d), lambda i: (0, 0), pipeline_mode=pl.Buffered(1)),
        ],
        out_specs=pl.BlockSpec((tm, d), lambda i: (i, 0)),
        out_shape=jax.ShapeDtypeStruct((m, d), F32),
        compiler_params=pltpu.CompilerParams(
            dimension_semantics=("parallel",), vmem_limit_bytes=VMEM_LIMIT),
        name="mem_out_proj",
    )(main, side, side, mkv, g_q.reshape(1, HEAD_DIM), g_k.reshape(1, HEAD_DIM), x, w)


def _rope_tables(pos_ref, c_ref):
    ang = pos_ref[...].astype(F32) * c_ref[0:1, :]
    cosv, sinv = jnp.cos(ang), jnp.sin(ang)
    return cosv * c_ref[1:2, :], sinv * c_ref[2:3, :], sinv * c_ref[3:4, :]


def _rope(x, tables):
    c, s1, s2 = tables
    return x * c + pltpu.roll(x, LANES - QK_ROPE_DIM // 2, 1) * s1 + pltpu.roll(x, QK_ROPE_DIM // 2, 1) * s2


def _kv_kernel(x_ref, pos_ref, c_ref, gx_ref, wd_ref, gc_ref, wu_ref, gkn_ref, gkr_ref, kx_ref, v_ref):
    x = x_ref[...]
    h = (x * _rms_scale(x) * gx_ref[...]).astype(BF16)
    c = _dot(h, wd_ref[...])
    ckv = c[:, :KV_LORA_RANK]
    cn = (ckv * _rms_scale(ckv) * gc_ref[...]).astype(BF16)
    kv = _dot(cn, wu_ref[...])
    kr = c[:, KV_LORA_RANK:]
    kr = kr * _rms_scale(kr, QK_ROPE_DIM) * gkr_ref[...]
    k_rope = _rope(kr, _rope_tables(pos_ref, c_ref)).astype(BF16)
    for hd in range(N_MLA_HEADS):
        base = hd * QK_PAD
        kn = kv[:, base:base + QK_NOPE_DIM]
        kx_ref[:, base:base + QK_NOPE_DIM] = (kn * _rms_scale(kn) * gkn_ref[...]).astype(BF16)
        kx_ref[:, base + QK_NOPE_DIM:base + QK_PAD] = k_rope
        v_ref[:, hd * V_HEAD_DIM:(hd + 1) * V_HEAD_DIM] = kv[:, base + QK_NOPE_DIM:base + QK_PAD].astype(BF16)


def _q_kernel(x_ref, pos_ref, c_ref, gx_ref, win_ref, gl_ref, wuq_ref, gqn_ref, gqr_ref, qx_ref, side_ref):
    x = x_ref[...]
    h = (x * _rms_scale(x) * gx_ref[...]).astype(BF16)
    ql = _dot(h, win_ref[:, :Q_LORA_RANK])
    qn = (ql * _rms_scale(ql) * gl_ref[...]).astype(BF16)
    q = _dot(qn, wuq_ref[...])
    tables = _rope_tables(pos_ref, c_ref)
    for hd in range(N_MLA_HEADS):
        base = hd * QK_PAD
        qnope = q[:, base:base + QK_NOPE_DIM]
        qx_ref[:, base:base + QK_NOPE_DIM] = (qnope * _rms_scale(qnope) * gqn_ref[...] * MLA_SCALE).astype(BF16)
        qr = q[:, base + QK_NOPE_DIM:base + QK_PAD]
        qr = qr * _rms_scale(qr, QK_ROPE_DIM) * gqr_ref[...]
        qx_ref[:, base + QK_NOPE_DIM:base + QK_PAD] = (_rope(qr, tables) * MLA_SCALE).astype(BF16)
    side_ref[...] = _dot(h, win_ref[:, Q_LORA_RANK:]).astype(BF16)


def _row_call(body, x, pos, consts, small, outs, tm, name):
    m, d = x.shape
    assert m % tm == 0, (m, tm)
    resident = [pl.BlockSpec(a.shape, lambda i: (0, 0), pipeline_mode=pl.Buffered(1)) for a in (consts, *small)]
    return pl.pallas_call(
        body,
        grid=(m // tm,),
        in_specs=[pl.BlockSpec((tm, d), lambda i: (i, 0)), pl.BlockSpec((tm, 1), lambda i: (i, 0))] + resident,
        out_specs=[pl.BlockSpec((tm, w), lambda i: (i, 0)) for w in outs],
        out_shape=[jax.ShapeDtypeStruct((m, w), BF16) for w in outs],
        compiler_params=pltpu.CompilerParams(
            dimension_semantics=("parallel",), vmem_limit_bytes=VMEM_LIMIT),
        name=name,
    )(x, pos, consts, *small)


def _rope_consts():
    half = QK_ROPE_DIM // 2
    inv_freq = jnp.power(ROPE_THETA, -jnp.arange(0, QK_ROPE_DIM, 2, dtype=F32) / QK_ROPE_DIM)
    lane = np.arange(LANES)
    rows = jnp.zeros((F32_ROWS, LANES), F32)
    rows = rows.at[0, :QK_ROPE_DIM].set(jnp.concatenate([inv_freq, inv_freq]))
    rows = rows.at[1].set(jnp.asarray(lane < QK_ROPE_DIM, F32))
    rows = rows.at[2].set(jnp.asarray(-(lane < half).astype(np.float32)))
    rows = rows.at[3].set(jnp.asarray(((lane >= half) & (lane < QK_ROPE_DIM)).astype(np.float32)))
    return rows


def _pad_cols(a, width):
    return jnp.pad(a, [(0, 0)] * (a.ndim - 1) + [(0, width - a.shape[-1])])


def kernel(x, mem, positions, a_norm, a_w_in, a_w_out, kv_norm, w_dkv, g_ckv, w_ukv, g_k_nope, g_k_rope, b_norm, b_w_in, b_g_q_lat, b_w_uq, b_g_q_nope, b_g_q_rope, b_w_out, mem_norm, w_mem_kv, g_mem_q, g_mem_k):
    batch, seq, d = x.shape
    m = batch * seq
    mem_len = mem.shape[1]
    x2 = x.reshape(m, d)
    mem2 = mem.reshape(batch * mem_len, d)
    pos = positions.reshape(m, 1)
    consts = _rope_consts()
    row = lambda g: g.reshape(1, -1)

    def only(w):
        assert w.shape[0] == 1, w.shape
        return w.reshape(w.shape[1:])

    def mem_kv(layer, w_layers):
        mkv = _norm_matmul(mem2, mem_norm[layer], w_layers, jnp.ones((2 * MEM_W,), F32), F32, *MEMKV_TILES,
                           w_row_block=layer)
        return mkv.reshape(batch, mem_len, 2 * MEM_W)

    a_in_w = a_w_in.shape[-1]
    qscale = jnp.ones((a_in_w,), F32).at[:SB_W].set(HEAD_DIM ** -0.5 * LOG2_E)
    pa = _norm_matmul(x2, a_norm[0], only(a_w_in).astype(BF16), qscale, BF16, INPROJ_TM, a_in_w // INPROJ_NSPLIT)
    sb, (a_wo, wd, wu, b_wi, wuq, b_wo, wm) = _sb_attention(
        pa, batch, seq,
        (only(a_w_out), w_dkv, w_ukv, only(b_w_in), only(b_w_uq), only(b_w_out), w_mem_kv.reshape(-1, 2 * MEM_W)))
    qm_blk = 4 * SB_W // MEM_W
    x2 = _mem_out_proj(sb, pa, qm_blk, qm_blk + 1, mem_kv(0, wm), g_mem_q[0], g_mem_k[0], x2, a_wo, seq)

    kx, v = _row_call(
        _kv_kernel, x2, pos, consts,
        (row(kv_norm), _pad_cols(wd, KV_LORA_RANK + LANES), row(g_ckv), wu, row(g_k_nope),
         _pad_cols(row(g_k_rope), LANES)),
        (N_MLA_HEADS * QK_PAD, MLA_W), KV_TM, "mla_kv_side")

    wuq = wuq.reshape(Q_LORA_RANK, N_MLA_HEADS, QK_NOPE_DIM + QK_ROPE_DIM)
    wuq = _pad_cols(wuq, QK_PAD).reshape(Q_LORA_RANK, N_MLA_HEADS * QK_PAD)
    qx, side = _row_call(
        _q_kernel, x2, pos, consts,
        (row(b_norm[0]), b_wi, row(b_g_q_lat[0]), wuq, row(b_g_q_nope[0]), _pad_cols(row(b_g_q_rope[0]), LANES)),
        (N_MLA_HEADS * QK_PAD, MLA_W + 2 * MEM_W), Q_TM, "mla_q_side")
    bound = _mla_score_bound(b_g_q_nope[0], b_g_q_rope[0], g_k_nope, g_k_rope)
    att = _mla_attention(bound, qx, kx, v, side, batch, seq)
    qm_blk = MLA_W // MEM_W
    x2 = _mem_out_proj(att, side, qm_blk, qm_blk + 1, mem_kv(1, wm), g_mem_q[1], g_mem_k[1], x2, b_wo, seq)
    return x2.reshape(batch, seq, d)
```

```python
import functools
import math

import jax
import jax.numpy as jnp
import numpy as np
from jax import lax
from jax.experimental import pallas as pl
from jax.experimental.pallas import tpu as pltpu

F32 = jnp.float32
BF16 = jnp.bfloat16

HEAD_DIM = 128
N_SB_HEADS = 12
N_MEM_HEADS = 4
N_MLA_HEADS = 12
Q_LORA_RANK = 512
KV_LORA_RANK = 512
QK_NOPE_DIM = 128
QK_ROPE_DIM = 64
V_HEAD_DIM = 128
ROPE_THETA = 10000.0
EPS = 1e-6
SB_W = N_SB_HEADS * HEAD_DIM
MEM_W = N_MEM_HEADS * HEAD_DIM
MLA_W = N_MLA_HEADS * V_HEAD_DIM
LOG2_E = math.log2(math.e)
LANES = 128
QK_PAD = 2 * LANES
VMEM_LIMIT = 56 * 1024 * 1024

INPROJ_TM, INPROJ_NSPLIT = 512, 2
MEMKV_TILES = (512, 1024)
OUT_TM = 512
KV_TM = 1024
Q_TM = 512
ROPE_TM = 2048
SB_TQ = 4096
MLA_TQ, MLA_TK = 2048, 512
SB_SUB = 2 * LANES
EXP2_CLAMP = 126.0
SB_DEAD_LOG2 = 150.0
SB_NO_KEYS_LEFT = 1e30
SB_CAST_SLABS = 32
BF16_ROWS = 16
F32_ROWS = 8
MLA_SAFE_BOUND = 60.0
MLA_SCALE = (QK_NOPE_DIM + QK_ROPE_DIM) ** -0.5 * LOG2_E
MLA_BOUND_SLACK = 1.02


def _rms_scale(x, width=None):
    ss = jnp.sum(x * x, axis=-1, keepdims=True)
    return lax.rsqrt(ss / (x.shape[-1] if width is None else width) + EPS)


def _silu(g):
    return g / (1.0 + jnp.exp(-g))


def _dot(a, b):
    return jnp.dot(a, b, preferred_element_type=F32)


def _dot_nt(a, b):
    return lax.dot_general(a, b, (((1,), (1,)), ((), ())), preferred_element_type=F32)


def _normmm_kernel(x_ref, g_ref, w_ref, cs_ref, o_ref, h_ref):
    @pl.when(pl.program_id(1) == 0)
    def _():
        x = x_ref[...]
        h_ref[...] = (x * _rms_scale(x) * g_ref[...]).astype(BF16)

    o_ref[...] = (_dot(h_ref[...], w_ref[...]) * cs_ref[...]).astype(o_ref.dtype)


def _norm_matmul(x, g, w, colscale, out_dtype, tm, tn, w_row_block=0):
    m, k = x.shape
    n = w.shape[1]
    assert m % tm == 0 and n % tn == 0 and w.shape[0] % k == 0, (m, n, tm, tn, w.shape)
    return pl.pallas_call(
        _normmm_kernel,
        grid=(m // tm, n // tn),
        in_specs=[
            pl.BlockSpec((tm, k), lambda i, j: (i, 0)),
            pl.BlockSpec((1, k), lambda i, j: (0, 0)),
            pl.BlockSpec((k, tn), lambda i, j: (w_row_block, j)),
            pl.BlockSpec((1, tn), lambda i, j: (0, j)),
        ],
        out_specs=pl.BlockSpec((tm, tn), lambda i, j: (i, j)),
        out_shape=jax.ShapeDtypeStruct((m, n), out_dtype),
        scratch_shapes=[pltpu.VMEM((tm, k), BF16)],
        compiler_params=pltpu.CompilerParams(
            dimension_semantics=("parallel", "arbitrary"), vmem_limit_bytes=VMEM_LIMIT),
        name="norm_matmul",
    )(x, g.reshape(1, k), w, colscale.reshape(1, n))


def _mask_top(x, keep, fill):
    n = keep.shape[0]
    top = jnp.where(keep, x[:n], fill)
    return top if x.shape[0] == n else jnp.concatenate([top, x[n:]], axis=0)


def _walk_key_blocks(block, i, nd):
    for d in reversed(range(nd)):
        block(i * nd + d, d, True)

    def body(j, carry):
        block(i * nd - 1 - j, 0, False)
        return carry

    lax.fori_loop(0, i * nd, body, 0)


def _sb_kernel(*refs, tq, n_cast):
    q_ref, k_ref, v_ref, g_ref, u_ref = refs[:5]
    o_ref = refs[5 + n_cast]
    acc_ref, carry_ref = refs[6 + 2 * n_cast:]
    for src_ref, dst_ref in zip(refs[5:5 + n_cast], refs[6 + n_cast:6 + 2 * n_cast]):
        dst_ref[...] = src_ref[...].astype(BF16)

    i = pl.program_id(2)
    nsub = tq // SB_SUB
    u = u_ref[...]
    row = lax.broadcasted_iota(jnp.int32, (SB_SUB, SB_SUB), 0)
    col = lax.broadcasted_iota(jnp.int32, (SB_SUB, SB_SUB), 1)
    causal = col < row
    acc_ref[...] = jnp.zeros_like(acc_ref)
    carry_ref[...] = jnp.zeros_like(carry_ref)

    def rows(r):
        return slice(r * SB_SUB, (r + 1) * SB_SUB)

    def per_sub_block(fn, x):
        return jnp.concatenate([fn(x[rows(r)]) for r in range(nsub)], axis=0)

    def step(o, diagonal):
        starts, zs = [], []
        for r in range(nsub):
            kb = i * nsub + r - o
            if not diagonal:
                carry_ref[rows(r), :] = jnp.where(kb >= 0, carry_ref[rows(r), :], SB_NO_KEYS_LEFT)
                kb = jnp.maximum(kb, 0)
            starts.append(pl.multiple_of(kb * SB_SUB, SB_SUB))
            zs.append(_dot_nt(q_ref[rows(r), :], k_ref[pl.ds(starts[r], SB_SUB), :]))
        z = jnp.concatenate(zs, axis=0)
        neg_log = jnp.maximum(z, jnp.log(1.0 + jnp.exp2(jnp.minimum(z, EXP2_CLAMP))) * LOG2_E)
        if diagonal:
            neg_log = per_sub_block(lambda t: jnp.where(causal, t, 0.0), neg_log)
        suffix = _dot(neg_log.astype(BF16), u)
        seen = carry_ref[...]
        a = jnp.exp2(z - neg_log - suffix - jnp.concatenate([seen] * (SB_SUB // LANES), axis=1))
        if diagonal:
            a = per_sub_block(lambda t: jnp.where(causal, t, 0.0), a)
        a = a.astype(BF16)
        for r in range(nsub):
            acc_ref[rows(r), :] += _dot(a[rows(r)], v_ref[pl.ds(starts[r], SB_SUB), :])
        carry_ref[...] = seen + jnp.sum(neg_log, axis=-1, keepdims=True)

    step(0, True)
    step(1, False)

    def more(state):
        o, lightest = state
        return jnp.logical_and(o < (i + 1) * nsub, lightest < SB_DEAD_LOG2)

    def advance(state):
        step(state[0], False)
        return state[0] + 1, jnp.min(carry_ref[...])

    lax.while_loop(more, advance, (2, jnp.min(carry_ref[...])))
    o_ref[...] = (acc_ref[...] * _silu(g_ref[...].astype(F32))).astype(o_ref.dtype)


def _sb_attention(pa, batch, seq, to_bf16):
    tq = SB_TQ
    assert seq % tq == 0 and tq % SB_SUB == 0, (seq, tq)
    nq = seq // tq
    h0 = N_SB_HEADS
    u = jnp.tril(jnp.ones((SB_SUB, SB_SUB), F32), -1).astype(BF16)
    assert batch * N_SB_HEADS * nq >= SB_CAST_SLABS

    def slab(w):
        assert w.shape[0] % (SB_CAST_SLABS * BF16_ROWS) == 0, w.shape
        step = lambda b, h, i: (jnp.minimum((b * N_SB_HEADS + h) * nq + i, SB_CAST_SLABS - 1), 0)
        return pl.BlockSpec((w.shape[0] // SB_CAST_SLABS, w.shape[1]), step)

    out, *cast = pl.pallas_call(
        functools.partial(_sb_kernel, tq=tq, n_cast=len(to_bf16)),
        grid=(batch, N_SB_HEADS, nq),
        in_specs=[
            pl.BlockSpec((tq, HEAD_DIM), lambda b, h, i: (b * nq + i, h)),
            pl.BlockSpec((seq, HEAD_DIM), lambda b, h, i: (b, h0 + h)),
            pl.BlockSpec((seq, HEAD_DIM), lambda b, h, i: (b, 2 * h0 + h)),
            pl.BlockSpec((tq, HEAD_DIM), lambda b, h, i: (b * nq + i, 3 * h0 + h)),
            pl.BlockSpec((SB_SUB, SB_SUB), lambda b, h, i: (0, 0)),
        ] + [slab(w) for w in to_bf16],
        out_specs=[pl.BlockSpec((tq, HEAD_DIM), lambda b, h, i: (b * nq + i, h))] + [slab(w) for w in to_bf16],
        out_shape=[jax.ShapeDtypeStruct((batch * seq, SB_W), BF16)]
        + [jax.ShapeDtypeStruct(w.shape, BF16) for w in to_bf16],
        scratch_shapes=[pltpu.VMEM((tq, HEAD_DIM), F32), pltpu.VMEM((tq, LANES), F32)],
        compiler_params=pltpu.CompilerParams(
            dimension_semantics=("arbitrary", "arbitrary", "arbitrary"), vmem_limit_bytes=VMEM_LIMIT),
        name="sb_attention",
    )(pa, pa, pa, pa, u, *to_bf16)
    return out, cast


def _mla_kernel(bound_ref, q_ref, k_ref, v_ref, g_ref, o_ref, acc_ref, m_ref, den_ref, *, tq, tk):
    i = pl.program_id(2)
    row = lax.broadcasted_iota(jnp.int32, (tk, tk), 0)
    col = lax.broadcasted_iota(jnp.int32, (tk, tk), 1)
    causal = col <= row
    nlane = tk // LANES
    nd = tq // tk
    bound = bound_ref[0]

    def scores(kb, d, diagonal):
        start = pl.multiple_of(kb * tk, tk)
        s = _dot_nt(q_ref[d * tk:, :], k_ref[pl.ds(start, tk), :])
        return _mask_top(s, causal, -jnp.inf) if diagonal else s

    def finish(den):
        o_ref[...] = (acc_ref[...] / den * _silu(g_ref[...].astype(F32))).astype(o_ref.dtype)

    def add(xs):
        return functools.reduce(lambda a, b: a + b, xs)

    def bounded():
        def tile(kb, d, diagonal):
            s = scores(kb, d, diagonal)
            ps = [jnp.exp2(s[:, c * LANES:(c + 1) * LANES] - bound) for c in range(nlane)]
            p = jnp.concatenate(ps, axis=1).astype(BF16)
            return add(ps), _dot(p, v_ref[pl.ds(pl.multiple_of(kb * tk, tk), tk), :])

        dens, pvs = zip(*[tile(i * nd + d, d, True) for d in range(nd)])
        for r in range(nd):
            rows = slice(r * tk, (r + 1) * tk)
            den_ref[rows, :] = add([dens[d][(r - d) * tk:(r - d + 1) * tk] for d in range(r + 1)])
            acc_ref[rows, :] = add([pvs[d][(r - d) * tk:(r - d + 1) * tk] for d in range(r + 1)])

        def older(j, carry):
            dens, pvs = zip(*[tile((i - j) * nd - 1 - u, 0, False) for u in range(nd)])
            den_ref[...] += add(dens)
            acc_ref[...] += add(pvs)
            return carry

        lax.fori_loop(0, i, older, 0)
        finish(jnp.sum(den_ref[...], axis=-1, keepdims=True))

    def online():
        acc_ref[...] = jnp.zeros_like(acc_ref)
        den_ref[...] = jnp.zeros_like(den_ref)
        m_ref[...] = jnp.full_like(m_ref, -jnp.inf)

        def block(kb, d, diagonal):
            r0 = d * tk
            s = scores(kb, d, diagonal)
            m_old = m_ref[r0:, :]
            m_new = jnp.maximum(m_old, jnp.max(s, axis=-1, keepdims=True))
            alpha = jnp.exp2(m_old - m_new)
            p = jnp.concatenate([jnp.exp2(s[:, c * LANES:(c + 1) * LANES] - m_new) for c in range(nlane)], axis=1)
            den_ref[r0:, :] = alpha * den_ref[r0:, :] + jnp.sum(p, axis=-1, keepdims=True)
            pv = _dot(p.astype(BF16), v_ref[pl.ds(pl.multiple_of(kb * tk, tk), tk), :])
            acc_ref[r0:, :] = alpha * acc_ref[r0:, :] + pv
            m_ref[r0:, :] = m_new

        _walk_key_blocks(block, i, nd)
        finish(den_ref[...])

    lax.cond(bound <= MLA_SAFE_BOUND, bounded, online)


def _mla_score_bound(g_q_nope, g_q_rope, g_k_nope, g_k_rope):
    def norm(g_nope, g_rope):
        return jnp.sqrt(QK_NOPE_DIM * jnp.max(g_nope * g_nope) + QK_ROPE_DIM * jnp.max(g_rope * g_rope))

    return (MLA_BOUND_SLACK * MLA_SCALE * norm(g_q_nope, g_q_rope) * norm(g_k_nope, g_k_rope)).reshape(1)


def _mla_attention(bound, qx, kx, v, gq, batch, seq):
    tq, tk = MLA_TQ, MLA_TK
    assert seq % tq == 0 and tq % tk == 0 and tk % LANES == 0, (seq, tq, tk)
    nq = seq // tq
    return pl.pallas_call(
        functools.partial(_mla_kernel, tq=tq, tk=tk),
        grid=(batch, N_MLA_HEADS, nq),
        in_specs=[
            pl.BlockSpec(memory_space=pltpu.SMEM),
            pl.BlockSpec((tq, QK_PAD), lambda b, h, i: (b * nq + i, h)),
            pl.BlockSpec((seq, QK_PAD), lambda b, h, i: (b, h)),
            pl.BlockSpec((seq, V_HEAD_DIM), lambda b, h, i: (b, h)),
            pl.BlockSpec((tq, V_HEAD_DIM), lambda b, h, i: (b * nq + i, h)),
        ],
        out_specs=pl.BlockSpec((tq, V_HEAD_DIM), lambda b, h, i: (b * nq + i, h)),
        out_shape=jax.ShapeDtypeStruct((batch * seq, MLA_W), BF16),
        scratch_shapes=[pltpu.VMEM((tq, V_HEAD_DIM), F32), pltpu.VMEM((tq, LANES), F32),
                        pltpu.VMEM((tq, LANES), F32)],
        compiler_params=pltpu.CompilerParams(
            dimension_semantics=("parallel", "parallel", "arbitrary"), vmem_limit_bytes=VMEM_LIMIT),
        name="mla_attention",
    )(bound, qx, kx, v, gq)


def _out_kernel(main_ref, qm_ref, gm_ref, mkv_ref, gq_ref, gk_ref, x_ref, w_ref, o_ref, *, main_w):
    heads = []
    for h in range(N_MEM_HEADS):
        lo, hi = h * HEAD_DIM, (h + 1) * HEAD_DIM
        mk = mkv_ref[0, :, lo:hi]
        mk = (mk * _rms_scale(mk) * gk_ref[...]).astype(BF16)
        mv = mkv_ref[0, :, MEM_W + lo:MEM_W + hi].astype(BF16)
        q = qm_ref[:, lo:hi].astype(F32)
        q = (q * _rms_scale(q) * gq_ref[...]).astype(BF16)
        s = _dot_nt(q, mk) * HEAD_DIM ** -0.5
        p = jnp.exp(s - jnp.max(s, axis=-1, keepdims=True))
        mo = _dot(p.astype(BF16), mv) / jnp.sum(p, axis=-1, keepdims=True)
        heads.append((mo * _silu(gm_ref[:, lo:hi].astype(F32))).astype(BF16))
    acc = _dot(main_ref[...], w_ref[:main_w, :]) + _dot(jnp.concatenate(heads, axis=1), w_ref[main_w:, :])
    o_ref[...] = x_ref[...] + acc


def _mem_out_proj(main, side, qm_blk, gm_blk, mkv, g_q, g_k, x, w, seq):
    m, main_w = main.shape
    d = w.shape[1]
    mem_len = mkv.shape[1]
    tm = OUT_TM
    assert seq % tm == 0, (seq, tm)
    per_b = seq // tm
    return pl.pallas_call(
        functools.partial(_out_kernel, main_w=main_w),
        grid=(m // tm,),
        in_specs=[
            pl.BlockSpec((tm, main_w), lambda i: (i, 0)),
            pl.BlockSpec((tm, MEM_W), lambda i: (i, qm_blk)),
            pl.BlockSpec((tm, MEM_W), lambda i: (i, gm_blk)),
            pl.BlockSpec((1, mem_len, 2 * MEM_W), lambda i: (i // per_b, 0, 0)),
            pl.BlockSpec((1, HEAD_DIM), lambda i: (0, 0)),
            pl.BlockSpec((1, HEAD_DIM), lambda i: (0, 0)),
            pl.BlockSpec((tm, d), lambda i: (i, 0)),
            pl.BlockSpec((main_w + MEM_W, d), lambda i: (0, 0), pipeline_mode=pl.Buffered(1)),
        ],
        out_specs=pl.BlockSpec((tm, d), lambda i: (i, 0)),
        out_shape=jax.ShapeDtypeStruct((m, d), F32),
        compiler_params=pltpu.CompilerParams(
            dimension_semantics=("parallel",), vmem_limit_bytes=VMEM_LIMIT),
        name="mem_out_proj",
    )(main, side, side, mkv, g_q.reshape(1, HEAD_DIM), g_k.reshape(1, HEAD_DIM), x, w)


def _rope_table_kernel(pos_ref, c_ref, o_ref):
    ang = pos_ref[...].astype(F32) * c_ref[0:1, :]
    cosv, sinv = jnp.cos(ang), jnp.sin(ang)
    o_ref[...] = jnp.concatenate([cosv * c_ref[1:2, :], sinv * c_ref[2:3, :], sinv * c_ref[3:4, :]], axis=1)


def _rope_table_call(pos, consts):
    m = pos.shape[0]
    assert m % ROPE_TM == 0, (m, ROPE_TM)
    return pl.pallas_call(
        _rope_table_kernel,
        grid=(m // ROPE_TM,),
        in_specs=[pl.BlockSpec((ROPE_TM, 1), lambda i: (i, 0)), pl.BlockSpec(consts.shape, lambda i: (0, 0))],
        out_specs=pl.BlockSpec((ROPE_TM, 3 * LANES), lambda i: (i, 0)),
        out_shape=jax.ShapeDtypeStruct((m, 3 * LANES), F32),
        compiler_params=pltpu.CompilerParams(dimension_semantics=("parallel",), vmem_limit_bytes=VMEM_LIMIT),
        name="rope_tables",
    )(pos, consts)


def _rope_tables(tab_ref):
    return tab_ref[:, :LANES], tab_ref[:, LANES:2 * LANES], tab_ref[:, 2 * LANES:]


def _rope(x, tables):
    c, s1, s2 = tables
    return x * c + pltpu.roll(x, LANES - QK_ROPE_DIM // 2, 1) * s1 + pltpu.roll(x, QK_ROPE_DIM // 2, 1) * s2


def _kv_kernel(x_ref, tab_ref, gx_ref, wd_ref, gc_ref, wu_ref, gkn_ref, gkr_ref, kx_ref, v_ref):
    x = x_ref[...]
    h = (x * _rms_scale(x) * gx_ref[...]).astype(BF16)
    c = _dot(h, wd_ref[...])
    ckv = c[:, :KV_LORA_RANK]
    cn = (ckv * _rms_scale(ckv) * gc_ref[...]).astype(BF16)
    kv = _dot(cn, wu_ref[...])
    kr = c[:, KV_LORA_RANK:]
    kr = kr * _rms_scale(kr, QK_ROPE_DIM) * gkr_ref[...]
    k_rope = _rope(kr, _rope_tables(tab_ref)).astype(BF16)
    for hd in range(N_MLA_HEADS):
        base = hd * QK_PAD
        kn = kv[:, base:base + QK_NOPE_DIM]
        kx_ref[:, base:base + QK_NOPE_DIM] = (kn * _rms_scale(kn) * gkn_ref[...]).astype(BF16)
        kx_ref[:, base + QK_NOPE_DIM:base + QK_PAD] = k_rope
        v_ref[:, hd * V_HEAD_DIM:(hd + 1) * V_HEAD_DIM] = kv[:, base + QK_NOPE_DIM:base + QK_PAD].astype(BF16)


def _q_kernel(x_ref, tab_ref, gx_ref, win_ref, gl_ref, wuq_ref, gqn_ref, gqr_ref, qx_ref, side_ref):
    x = x_ref[...]
    h = (x * _rms_scale(x) * gx_ref[...]).astype(BF16)
    ql = _dot(h, win_ref[:, :Q_LORA_RANK])
    qn = (ql * _rms_scale(ql) * gl_ref[...]).astype(BF16)
    q = _dot(qn, wuq_ref[...])
    tables = _rope_tables(tab_ref)
    for hd in range(N_MLA_HEADS):
        base = hd * QK_PAD
        qnope = q[:, base:base + QK_NOPE_DIM]
        qx_ref[:, base:base + QK_NOPE_DIM] = (qnope * _rms_scale(qnope) * gqn_ref[...] * MLA_SCALE).astype(BF16)
        qr = q[:, base + QK_NOPE_DIM:base + QK_PAD]
        qr = qr * _rms_scale(qr, QK_ROPE_DIM) * gqr_ref[...]
        qx_ref[:, base + QK_NOPE_DIM:base + QK_PAD] = (_rope(qr, tables) * MLA_SCALE).astype(BF16)
    side_ref[...] = _dot(h, win_ref[:, Q_LORA_RANK:]).astype(BF16)


def _row_call(body, x, tab, small, outs, tm, name):
    m, d = x.shape
    assert m % tm == 0, (m, tm)
    resident = [pl.BlockSpec(a.shape, lambda i: (0, 0), pipeline_mode=pl.Buffered(1)) for a in small]
    return pl.pallas_call(
        body,
        grid=(m // tm,),
        in_specs=[pl.BlockSpec((tm, d), lambda i: (i, 0)),
                  pl.BlockSpec((tm, tab.shape[1]), lambda i: (i, 0))] + resident,
        out_specs=[pl.BlockSpec((tm, w), lambda i: (i, 0)) for w in outs],
        out_shape=[jax.ShapeDtypeStruct((m, w), BF16) for w in outs],
        compiler_params=pltpu.CompilerParams(
            dimension_semantics=("parallel",), vmem_limit_bytes=VMEM_LIMIT),
        name=name,
    )(x, tab, *small)


def _rope_consts():
    half = QK_ROPE_DIM // 2
    inv_freq = jnp.power(ROPE_THETA, -jnp.arange(0, QK_ROPE_DIM, 2, dtype=F32) / QK_ROPE_DIM)
    lane = np.arange(LANES)
    rows = jnp.zeros((F32_ROWS, LANES), F32)
    rows = rows.at[0, :QK_ROPE_DIM].set(jnp.concatenate([inv_freq, inv_freq]))
    rows = rows.at[1].set(jnp.asarray(lane < QK_ROPE_DIM, F32))
    rows = rows.at[2].set(jnp.asarray(-(lane < half).astype(np.float32)))
    rows = rows.at[3].set(jnp.asarray(((lane >= half) & (lane < QK_ROPE_DIM)).astype(np.float32)))
    return rows


def _pad_cols(a, width):
    return jnp.pad(a, [(0, 0)] * (a.ndim - 1) + [(0, width - a.shape[-1])])


def kernel(x, mem, positions, a_norm, a_w_in, a_w_out, kv_norm, w_dkv, g_ckv, w_ukv, g_k_nope, g_k_rope, b_norm, b_w_in, b_g_q_lat, b_w_uq, b_g_q_nope, b_g_q_rope, b_w_out, mem_norm, w_mem_kv, g_mem_q, g_mem_k):
    batch, seq, d = x.shape
    m = batch * seq
    mem_len = mem.shape[1]
    x2 = x.reshape(m, d)
    mem2 = mem.reshape(batch * mem_len, d)
    pos = positions.reshape(m, 1)
    tab = _rope_table_call(pos, _rope_consts())
    row = lambda g: g.reshape(1, -1)

    def only(w):
        assert w.shape[0] == 1, w.shape
        return w.reshape(w.shape[1:])

    def mem_kv(layer, w_layers):
        mkv = _norm_matmul(mem2, mem_norm[layer], w_layers, jnp.ones((2 * MEM_W,), F32), F32, *MEMKV_TILES,
                           w_row_block=layer)
        return mkv.reshape(batch, mem_len, 2 * MEM_W)

    a_in_w = a_w_in.shape[-1]
    qscale = jnp.ones((a_in_w,), F32).at[:SB_W].set(HEAD_DIM ** -0.5 * LOG2_E)
    pa = _norm_matmul(x2, a_norm[0], only(a_w_in).astype(BF16), qscale, BF16, INPROJ_TM, a_in_w // INPROJ_NSPLIT)
    sb, (a_wo, wd, wu, b_wi, wuq, b_wo, wm) = _sb_attention(
        pa, batch, seq,
        (only(a_w_out), w_dkv, w_ukv, only(b_w_in), only(b_w_uq), only(b_w_out), w_mem_kv.reshape(-1, 2 * MEM_W)))
    qm_blk = 4 * SB_W // MEM_W
    x2 = _mem_out_proj(sb, pa, qm_blk, qm_blk + 1, mem_kv(0, wm), g_mem_q[0], g_mem_k[0], x2, a_wo, seq)

    kx, v = _row_call(
        _kv_kernel, x2, tab,
        (row(kv_norm), _pad_cols(wd, KV_LORA_RANK + LANES), row(g_ckv), wu, row(g_k_nope),
         _pad_cols(row(g_k_rope), LANES)),
        (N_MLA_HEADS * QK_PAD, MLA_W), KV_TM, "mla_kv_side")

    wuq = wuq.reshape(Q_LORA_RANK, N_MLA_HEADS, QK_NOPE_DIM + QK_ROPE_DIM)
    wuq = _pad_cols(wuq, QK_PAD).reshape(Q_LORA_RANK, N_MLA_HEADS * QK_PAD)
    qx, side = _row_call(
        _q_kernel, x2, tab,
        (row(b_norm[0]), b_wi, row(b_g_q_lat[0]), wuq, row(b_g_q_nope[0]), _pad_cols(row(b_g_q_rope[0]), LANES)),
        (N_MLA_HEADS * QK_PAD, MLA_W + 2 * MEM_W), Q_TM, "mla_q_side")
    bound = _mla_score_bound(b_g_q_nope[0], b_g_q_rope[0], g_k_nope, g_k_rope)
    att = _mla_attention(bound, qx, kx, v, side, batch, seq)
    qm_blk = MLA_W // MEM_W
    x2 = _mem_out_proj(att, side, qm_blk, qm_blk + 1, mem_kv(1, wm), g_mem_q[1], g_mem_k[1], x2, b_wo, seq)
    return x2.reshape(batch, seq, d)
```
